```python
import jax, jax.numpy as jnp
from jax import lax
import numpy as np

D_MODEL = 1024
BATCH = 16
SEQ = 2048
DEPTH = 4

GRID_W = 64
CTX_LEN = 256

MLA_HEADS = 8
MLA_Q_LORA = 384
MLA_KV_LORA = 256
MLA_NOPE = 64
MLA_ROPE = 32
MLA_V = 64
MLA_WIDTH = MLA_HEADS * MLA_V
ROPE_BASE = 10000.0
Q_BLOCK = 128

CONV_WIDTH = 512
CONV_K = 3

RW_HEADS = 8
RW_HEAD = 64
RW_WIDTH = RW_HEADS * RW_HEAD
RW_DECAY_LORA = 64
RW_ICLR_LORA = 64
RW_GATE_LORA = 128
RW_GN_EPS = 64e-5
RW_IN = 3 * RW_WIDTH + 2 * RW_DECAY_LORA + 2 * RW_ICLR_LORA + RW_GATE_LORA

N_BRANCH = 3
IN_SPLITS = (MLA_Q_LORA, MLA_KV_LORA, MLA_ROPE, CONV_WIDTH, CONV_WIDTH, CONV_WIDTH, RW_IN, N_BRANCH * D_MODEL)
D_IN = MLA_Q_LORA + MLA_KV_LORA + MLA_ROPE + 3 * CONV_WIDTH + RW_IN + N_BRANCH * D_MODEL

D_FF = -(-(8 * D_MODEL) // (3 * 256)) * 256

LN_EPS = 1e-5
RMS_EPS = 1e-6

kernel_name = "hybrid_mla_conv_rwkv7_dit_prefix"

F32 = jnp.float32


def _split(x, sizes):
    offs = np.cumsum(sizes)[:-1].tolist()
    return jnp.split(x, offs, axis=-1)


def layer_norm(x, g, b):
    xf = x.astype(F32)
    mu = jnp.mean(xf, -1, keepdims=True)
    var = jnp.mean(jnp.square(xf - mu), -1, keepdims=True)
    return ((xf - mu) * lax.rsqrt(var + LN_EPS) * g + b).astype(x.dtype)


def rms_norm(x, g):
    xf = x.astype(F32)
    return (xf * lax.rsqrt(jnp.mean(jnp.square(xf), -1, keepdims=True) + RMS_EPS) * g).astype(x.dtype)


def modulate(x, shift, scale):
    return x * (1.0 + scale) + shift


def axial_rope_angles(seq_len):
    rows = seq_len // GRID_W
    row = jnp.repeat(jnp.arange(rows), GRID_W).astype(F32)
    col = jnp.tile(jnp.arange(GRID_W), rows).astype(F32)
    axis_dim = MLA_ROPE // 2
    inv = ROPE_BASE ** (-jnp.arange(0, axis_dim, 2, dtype=F32) / axis_dim)
    return row[:, None] * inv, col[:, None] * inv


def _rot(x, ang):
    x1, x2 = jnp.split(x, 2, axis=-1)
    cos, sin = jnp.cos(ang).astype(x.dtype), jnp.sin(ang).astype(x.dtype)
    return jnp.concatenate([x1 * cos - x2 * sin, x1 * sin + x2 * cos], axis=-1)


def axial_rope(x, ang_row, ang_col):
    extra = (1,) * (x.ndim - 3)
    ar = ang_row.reshape(ang_row.shape[0], *extra, ang_row.shape[-1])
    ac = ang_col.reshape(ang_col.shape[0], *extra, ang_col.shape[-1])
    xr, xc = jnp.split(x, 2, axis=-1)
    return jnp.concatenate([_rot(xr, ar), _rot(xc, ac)], axis=-1)


def mla_project(cq, ckv, krope, q_norm, w_uq, kv_norm, w_ukv, angles):
    B, T, _ = cq.shape
    q = (rms_norm(cq, q_norm) @ w_uq).reshape(B, T, MLA_HEADS, MLA_NOPE + MLA_ROPE)
    kv = (rms_norm(ckv, kv_norm) @ w_ukv).reshape(B, T, MLA_HEADS, MLA_NOPE + MLA_V)
    q_nope, q_rope = q[..., :MLA_NOPE], q[..., MLA_NOPE:]
    k_nope, v = kv[..., :MLA_NOPE], kv[..., MLA_NOPE:]
    if angles is not None:
        q_rope = axial_rope(q_rope, *angles)
        krope = axial_rope(krope, *angles)
    k_rope = jnp.broadcast_to(krope[:, :, None, :], (B, T, MLA_HEADS, MLA_ROPE))
    return (jnp.concatenate([q_nope, q_rope], -1), jnp.concatenate([k_nope, k_rope], -1), v)


def attention(q, k, v):
    s = jnp.einsum('bqhd,bkhd->bhqk', q, k, preferred_element_type=F32) * (MLA_NOPE + MLA_ROPE) ** -0.5
    p = jax.nn.softmax(s, axis=-1).astype(v.dtype)
    return jnp.einsum('bhqk,bkhd->bqhd', p, v)


def blocked_attention(q, k, v):
    B, T, H, D = q.shape
    nb = T // Q_BLOCK
    qb = q.reshape(B, nb, Q_BLOCK, H, D).swapaxes(0, 1)
    ob = lax.map(lambda qi: attention(qi, k, v), qb)
    return ob.swapaxes(0, 1).reshape(B, T, H * v.shape[-1])


def _pad_seq(u):
    return jnp.pad(u, ((0, 0), (1, 1), (0, 0)))


def short_conv(h, gate_b, gate_c, w):
    up = _pad_seq(gate_c * h)
    return gate_b * (up[:, :-2] * w[0] + up[:, 1:-1] * w[1] + up[:, 2:] * w[2])


def _heads(t):
    return t.reshape(*t.shape[:-1], RW_HEADS, RW_HEAD)


def rwkv_features(part, mu, w0, w_up, a0, a_up, g_up, k_k, k_a):
    B, T, _ = part.shape
    up = _pad_seq(part)
    part = part + (0.5 * (up[:, :-2] + up[:, 2:]) - part) * mu
    r, k, v, wd, ad, gd = _split(part, (RW_WIDTH, RW_WIDTH, RW_WIDTH, 2 * RW_DECAY_LORA, 2 * RW_ICLR_LORA, RW_GATE_LORA))
    wd = wd.reshape(B, T, 2, RW_DECAY_LORA)
    ad = ad.reshape(B, T, 2, RW_ICLR_LORA)
    w_log = -jax.nn.softplus(-(w0 + jnp.einsum('btzl,zlc->btzc', jnp.tanh(wd), w_up))) - 0.5
    decay = jnp.exp(-jnp.exp(w_log.astype(F32)))
    a = jax.nn.sigmoid(a0 + jnp.einsum('btzl,zlc->btzc', ad, a_up))
    g = jax.nn.sigmoid(gd) @ g_up
    kkf = _heads(k * k_k).astype(F32)
    kk = kkf * lax.rsqrt(jnp.maximum(jnp.sum(jnp.square(kkf), -1, keepdims=True), 1e-24))
    k_dir = k[:, :, None, :] * (1.0 + (a - 1.0) * k_a)
    return (r, k, v, g, kk, decay, a, k_dir)


def wkv_scan(state, r, w, k, v, kk, a, reverse):
    xs = tuple(jnp.moveaxis(t.astype(F32), 1, 0) for t in (r, w, k, v, kk, a))

    def step(S, inp):
        r_t, w_t, k_t, v_t, kk_t, a_t = inp
        s_kk = jnp.einsum('bhvk,bhk->bhv', S, kk_t)
        S = S * w_t[:, :, None, :] - s_kk[..., None] * (kk_t * a_t)[:, :, None, :] + v_t[..., None] * k_t[:, :, None, :]
        return S, jnp.einsum('bhvk,bhk->bhv', S, r_t)

    S, ys = lax.scan(step, state, xs, reverse=reverse)
    return S, jnp.moveaxis(ys, 0, 1)


def wkv_direction(state, feats, z, reverse):
    r, k, v, g, kk, decay, a, k_dir = feats
    return wkv_scan(state, _heads(r), _heads(decay[:, :, z]), _heads(k_dir[:, :, z]), _heads(v), kk,
                    _heads(a[:, :, z]), reverse)


def rwkv_readout(y, feats, r_k, gn_g, gn_b):
    r, k, v, g = feats[:4]
    B, T = r.shape[:2]
    mu = jnp.mean(y, -1, keepdims=True)
    var = jnp.mean(jnp.square(y - mu), -1, keepdims=True)
    yn = ((y - mu) * lax.rsqrt(var + RW_GN_EPS)).reshape(B, T, RW_WIDTH) * gn_g + gn_b
    bonus = jnp.sum(_heads(r * k * r_k).astype(F32), -1, keepdims=True) * _heads(v).astype(F32)
    return ((yn + bonus.reshape(B, T, RW_WIDTH)) * g).astype(r.dtype)


def gated_merge(gate_logits, br_a, br_b, br_c):
    gates = jax.nn.sigmoid(gate_logits).reshape(*gate_logits.shape[:-1], N_BRANCH, D_MODEL)
    return gates[..., 0, :] * br_a + gates[..., 1, :] * br_b + gates[..., 2, :] * br_c


def swiglu(h, w13, w2):
    u, gt = jnp.split(h @ w13, 2, axis=-1)
    return (jax.nn.silu(u) * gt) @ w2


def setup_inputs(seed: int = 0) -> dict:
    key = jax.random.key(seed)
    ks = iter(jax.random.split(key, 40))
    L, D = DEPTH, D_MODEL
    beta = (8.0 * DEPTH) ** -0.25

    def nrm(shape, scale):
        return jax.random.normal(next(ks), shape, F32) * scale

    return {
        "x": nrm((BATCH, SEQ, D), 1.0),
        "c": nrm((BATCH, D), 1.0),
        "ctx": nrm((BATCH, CTX_LEN, D), 1.0),
        "c_ctx": nrm((D,), 1.0),
        "mod_w": nrm((L, D, 6 * D), 0.5 * D ** -0.5),
        "mod_b": nrm((L, 6 * D), 0.02),
        "w_in": nrm((L, D, D_IN), D ** -0.5),
        "q_norm": 1.0 + nrm((L, MLA_Q_LORA), 0.02),
        "w_uq": nrm((L, MLA_Q_LORA, MLA_HEADS * (MLA_NOPE + MLA_ROPE)), MLA_Q_LORA ** -0.5),
        "kv_norm": 1.0 + nrm((L, MLA_KV_LORA), 0.02),
        "w_ukv": nrm((L, MLA_KV_LORA, MLA_HEADS * (MLA_NOPE + MLA_V)), MLA_KV_LORA ** -0.5),
        "w_o_attn": nrm((L, MLA_WIDTH, D), MLA_WIDTH ** -0.5),
        "conv_w": nrm((L, CONV_K, CONV_WIDTH), CONV_K ** -0.5),
        "w_o_conv": nrm((L, CONV_WIDTH, D), CONV_WIDTH ** -0.5),
        "rw_mu": jax.random.uniform(next(ks), (L, RW_IN), F32),
        "rw_w0": jax.random.uniform(next(ks), (L, 2, RW_WIDTH), F32, -6.0, 1.0),
        "rw_w_up": nrm((L, 2, RW_DECAY_LORA, RW_WIDTH), 0.1),
        "rw_a0": nrm((L, 2, RW_WIDTH), 0.5),
        "rw_a_up": nrm((L, 2, RW_ICLR_LORA, RW_WIDTH), RW_ICLR_LORA ** -0.5),
        "rw_g_up": nrm((L, RW_GATE_LORA, RW_WIDTH), RW_GATE_LORA ** -0.5),
        "rw_k_k": 0.85 + nrm((L, RW_WIDTH), 0.02),
        "rw_k_a": 1.0 + nrm((L, RW_WIDTH), 0.02),
        "rw_r_k": nrm((L, RW_WIDTH), 0.1),
        "rw_gn_g": 1.0 + nrm((L, RW_WIDTH), 0.02),
        "rw_gn_b": nrm((L, RW_WIDTH), 0.02),
        "w_o_rwkv": nrm((L, RW_WIDTH, D), RW_WIDTH ** -0.5),
        "w_out": nrm((L, D, D), beta * D ** -0.5),
        "ln1_g": 1.0 + nrm((L, D), 0.02),
        "ln1_b": nrm((L, D), 0.02),
        "ffn_w13": nrm((L, D, 2 * D_FF), D ** -0.5),
        "ffn_w2": nrm((L, D_FF, D), beta * D_FF ** -0.5),
        "ln2_g": 1.0 + nrm((L, D), 0.02),
        "ln2_b": nrm((L, D), 0.02),
    }


def reference(x, c, ctx, c_ctx, mod_w, mod_b, w_in, q_norm, w_uq, kv_norm, w_ukv, w_o_attn,
              conv_w, w_o_conv, rw_mu, rw_w0, rw_w_up, rw_a0, rw_a_up, rw_g_up, rw_k_k, rw_k_a,
              rw_r_k, rw_gn_g, rw_gn_b, w_o_rwkv, w_out, ln1_g, ln1_b, ffn_w13, ffn_w2, ln2_g, ln2_b):
    B, T, _ = x.shape
    angles = axial_rope_angles(T)
    alpha = (2.0 * DEPTH) ** 0.25
    s_lat = jax.nn.silu(c)
    s_ctx = jax.nn.silu(c_ctx)
    xc = ctx
    for l in range(DEPTH):
        last = l == DEPTH - 1
        sh1, sc1, g1, sh2, sc2, g2 = jnp.split((s_lat @ mod_w[l] + mod_b[l])[:, None, :], 6, axis=-1)
        csh1, csc1, cg1, csh2, csc2, cg2 = jnp.split(s_ctx @ mod_w[l] + mod_b[l], 6, axis=-1)

        cq_l, ckv_l, kr_l, ch_l, cb_l, cc_l, rw_l, gt_l = _split(modulate(x, sh1, sc1) @ w_in[l], IN_SPLITS)
        cq_c, ckv_c, kr_c, ch_c, cb_c, cc_c, rw_c, gt_c = _split(modulate(xc, csh1, csc1) @ w_in[l], IN_SPLITS)

        q_l, k_l, v_l = mla_project(cq_l, ckv_l, kr_l, q_norm[l], w_uq[l], kv_norm[l], w_ukv[l], angles)
        q_c, k_c, v_c = mla_project(cq_c, ckv_c, kr_c, q_norm[l], w_uq[l], kv_norm[l], w_ukv[l], None)
        att_l = blocked_attention(q_l, jnp.concatenate([k_c, k_l], 1), jnp.concatenate([v_c, v_l], 1))

        rw_par = (rw_mu[l], rw_w0[l], rw_w_up[l], rw_a0[l], rw_a_up[l], rw_g_up[l], rw_k_k[l], rw_k_a[l])
        feats_c = rwkv_features(rw_c, *rw_par)
        feats_l = rwkv_features(rw_l, *rw_par)
        zero_state = jnp.zeros((B, RW_HEADS, RW_HEAD, RW_HEAD), F32)
        sf_c, yf_c = wkv_direction(zero_state, feats_c, 0, False)
        sb_c, yb_c = wkv_direction(zero_state, feats_c, 1, True)
        _, yf_l = wkv_direction(sf_c, feats_l, 0, False)
        _, yb_l = wkv_direction(sb_c, feats_l, 1, True)
        rwo_l = rwkv_readout(yf_l + yb_l, feats_l, rw_r_k[l], rw_gn_g[l], rw_gn_b[l])

        o_l = gated_merge(gt_l, att_l @ w_o_attn[l],
                          short_conv(ch_l, cb_l, cc_l, conv_w[l]) @ w_o_conv[l],
                          rwo_l @ w_o_rwkv[l]) @ w_out[l]
        x_new = layer_norm(alpha * x + g1 * o_l, ln1_g[l], ln1_b[l])
        x_new = layer_norm(alpha * x_new + g2 * swiglu(modulate(x_new, sh2, sc2), ffn_w13[l], ffn_w2[l]),
                           ln2_g[l], ln2_b[l])

        if not last:
            att_c = attention(q_c, k_c, v_c).reshape(B, xc.shape[1], MLA_WIDTH)
            rwo_c = rwkv_readout(yf_c + yb_c, feats_c, rw_r_k[l], rw_gn_g[l], rw_gn_b[l])
            o_c = gated_merge(gt_c, att_c @ w_o_attn[l],
                              short_conv(ch_c, cb_c, cc_c, conv_w[l]) @ w_o_conv[l],
                              rwo_c @ w_o_rwkv[l]) @ w_out[l]
            xc_new = layer_norm(alpha * xc + cg1 * o_c, ln1_g[l], ln1_b[l])
            xc = layer_norm(alpha * xc_new + cg2 * swiglu(modulate(xc_new, csh2, csc2), ffn_w13[l], ffn_w2[l]),
                            ln2_g[l], ln2_b[l])
        x = x_new
    return x
```

```python
import functools

import numpy as np
import jax
import jax.numpy as jnp
from jax import lax
from jax.experimental import pallas as pl
from jax.experimental.pallas import tpu as pltpu

F32 = jnp.float32
BF16 = jnp.bfloat16

D_MODEL = 1024
GRID_W = 64
MLA_HEADS = 8
MLA_Q_LORA = 384
MLA_KV_LORA = 256
MLA_NOPE = 64
MLA_ROPE = 32
MLA_V = 64
MLA_WIDTH = MLA_HEADS * MLA_V
ROPE_BASE = 10000.0
CONV_WIDTH = 512
RW_HEADS = 8
RW_HEAD = 64
RW_WIDTH = RW_HEADS * RW_HEAD
RW_DECAY_LORA = 64
RW_ICLR_LORA = 64
RW_GATE_LORA = 128
RW_GN_EPS = 64e-5
RW_IN = 3 * RW_WIDTH + 2 * RW_DECAY_LORA + 2 * RW_ICLR_LORA + RW_GATE_LORA
N_BRANCH = 3
D_FF = 2816
LN_EPS = 1e-5
RMS_EPS = 1e-6

LANES = 128
HEAD_PAD = LANES
TM = 256
CHUNK = 64
HALO = 8
VMEM_LIMIT = 56 * 1024 * 1024

MLA_SEG = MLA_Q_LORA + MLA_KV_LORA + 2 * HEAD_PAD
CONV_SEG = 3 * CONV_WIDTH
GATE_SEG = N_BRANCH * D_MODEL
IN_SEGS = (MLA_SEG, CONV_SEG, RW_IN, GATE_SEG)
IN_OFFS = tuple(int(v) for v in np.cumsum((0,) + IN_SEGS))


def _bdot(a, b):
    return jnp.dot(a.astype(BF16), b.astype(BF16), preferred_element_type=F32)


def _split2(a):
    hi = a.astype(BF16)
    lo = (a - hi.astype(F32)).astype(BF16)
    return hi, lo


def _dot3(a, b, dims=None):
    ah, al = _split2(a)
    bh, bl = _split2(b)
    if dims is None:
        f = lambda u, v: jnp.dot(u, v, preferred_element_type=F32)
    else:
        f = lambda u, v: lax.dot_general(u, v, dims, preferred_element_type=F32)
    return f(ah, bh) + (f(ah, bl) + f(al, bh))


def _dot_exact_rhs(a, b_bf16):
    a1 = a.astype(BF16)
    r1 = a - a1.astype(F32)
    a2 = r1.astype(BF16)
    a3 = (r1 - a2.astype(F32)).astype(BF16)
    f = lambda u: jnp.dot(u, b_bf16, preferred_element_type=F32)
    return f(a1) + (f(a2) + f(a3))


def _sigmoid(x):
    return 1.0 / (1.0 + jnp.exp(-x))


def _layer_norm(x, g, b):
    mu = jnp.mean(x, -1, keepdims=True)
    xc = x - mu
    var = jnp.mean(xc * xc, -1, keepdims=True)
    return xc * lax.rsqrt(var + LN_EPS) * g + b


def _const_spec(shape, index_map):
    return pl.BlockSpec(shape, index_map, pipeline_mode=pl.Buffered(1))


def _params(sem):
    return pltpu.CompilerParams(dimension_semantics=sem, vmem_limit_bytes=VMEM_LIMIT)


def _mod_kernel(c_ref, w_ref, b_ref, o_ref):
    c = c_ref[...]
    s = c * _sigmoid(c)
    o_ref[...] = _dot3(s, w_ref[...]) + b_ref[...]


def _mod_vectors(c_all, mod_w, mod_b):
    L = mod_w.shape[0]
    R = c_all.shape[0]
    n = mod_w.shape[2] // D_MODEL
    return pl.pallas_call(
        _mod_kernel,
        grid=(L, n),
        in_specs=[
            pl.BlockSpec((R, D_MODEL), lambda l, j: (0, 0)),
            pl.BlockSpec((None, D_MODEL, D_MODEL), lambda l, j: (l, 0, j)),
            pl.BlockSpec((None, 1, D_MODEL), lambda l, j: (l, 0, j)),
        ],
        out_specs=pl.BlockSpec((None, R, D_MODEL), lambda l, j: (l, 0, j)),
        out_shape=jax.ShapeDtypeStruct((L, R, n * D_MODEL), F32),
        compiler_params=_params(("parallel", "parallel")),
        name="mod_vectors",
    )(c_all, mod_w, mod_b.reshape(L, 1, -1))


class _Geom:
    def __init__(self, B, ctx_len, seq):
        assert ctx_len % TM == 0 and seq % TM == 0 and seq % GRID_W == 0
        self.B, self.ctx_len, self.seq = B, ctx_len, seq
        self.tt = ctx_len + seq
        self.nct = ctx_len // TM
        self.nt = self.tt // TM
        self.n = B * self.tt

    def mod_row(self, b, t):
        return jnp.where(t < self.nct, self.B, b)


def _mod_spec(geom, layer, j):
    return pl.BlockSpec((None, None, 1, D_MODEL), lambda b, t: (layer, geom.mod_row(b, t), 0, j))


def _tok_spec(geom, width):
    return pl.BlockSpec((TM, width), lambda b, t: (b * geom.nt + t, 0))


def _in_proj_kernel(x_ref, sh_ref, sc_ref, w_ref, mla_ref, conv_ref, rw_ref, gate_ref):
    h = (x_ref[...] * (1.0 + sc_ref[...]) + sh_ref[...]).astype(BF16)
    for o_ref, lo, hi in zip((mla_ref, conv_ref, rw_ref, gate_ref), IN_OFFS[:-1], IN_OFFS[1:]):
        o_ref[...] = jnp.dot(h, w_ref[:, lo:hi], preferred_element_type=F32)


def _in_proj(geom, layer, x, modv, w_in_p):
    return pl.pallas_call(
        _in_proj_kernel,
        grid=(geom.B, geom.nt),
        in_specs=[
            _tok_spec(geom, D_MODEL),
            _mod_spec(geom, layer, 0),
            _mod_spec(geom, layer, 1),
            _const_spec((None, D_MODEL, IN_OFFS[-1]), lambda b, t: (layer, 0, 0)),
        ],
        out_specs=[_tok_spec(geom, w) for w in IN_SEGS],
        out_shape=[jax.ShapeDtypeStruct((geom.n, w), F32) for w in IN_SEGS],
        compiler_params=_params(("parallel", "parallel")),
        name="in_proj",
    )(x, modv, modv, w_in_p)


def _mla_kernel(m_ref, qn_ref, kvn_ref, wq_ref, wqs_ref, wk_ref, wv_ref, cos_ref, sin_ref,
                q_ref, k_ref, v_ref):
    m = m_ref[...]
    cq = m[:, :MLA_Q_LORA]
    ckv = m[:, MLA_Q_LORA:MLA_Q_LORA + MLA_KV_LORA]
    kslab = m[:, MLA_Q_LORA + MLA_KV_LORA:MLA_Q_LORA + MLA_KV_LORA + HEAD_PAD]
    kslab_sw = m[:, MLA_Q_LORA + MLA_KV_LORA + HEAD_PAD:]
    cqn = (cq * lax.rsqrt(jnp.mean(cq * cq, -1, keepdims=True) + RMS_EPS) * qn_ref[...]).astype(BF16)
    ckvn = (ckv * lax.rsqrt(jnp.mean(ckv * ckv, -1, keepdims=True) + RMS_EPS) * kvn_ref[...]).astype(BF16)
    cos_t = cos_ref[...]
    sin_t = sin_ref[...]
    qa = jnp.dot(cqn, wq_ref[...], preferred_element_type=F32)
    qb = jnp.dot(cqn, wqs_ref[...], preferred_element_type=F32)
    kn = jnp.dot(ckvn, wk_ref[...], preferred_element_type=F32)
    kr = kslab * cos_t + kslab_sw * sin_t
    scale = (MLA_NOPE + MLA_ROPE) ** -0.5
    for h in range(MLA_HEADS):
        sl = slice(h * HEAD_PAD, (h + 1) * HEAD_PAD)
        q_ref[:, sl] = ((qa[:, sl] * cos_t + qb[:, sl] * sin_t) * scale).astype(BF16)
        k_ref[:, sl] = (kn[:, sl] + kr).astype(BF16)
    v_ref[...] = jnp.dot(ckvn, wv_ref[...], preferred_element_type=F32).astype(BF16)


def _mla_proj(geom, layer, mla, q_norm, kv_norm, wq, wqs, wk, wv, cos_t, sin_t):
    hw = MLA_HEADS * HEAD_PAD
    lsel = lambda b, t: (layer, 0, 0)
    return pl.pallas_call(
        _mla_kernel,
        grid=(geom.B, geom.nt),
        in_specs=[
            _tok_spec(geom, MLA_SEG),
            pl.BlockSpec((None, 1, MLA_Q_LORA), lsel),
            pl.BlockSpec((None, 1, MLA_KV_LORA), lsel),
            _const_spec((None, MLA_Q_LORA, hw), lsel),
            _const_spec((None, MLA_Q_LORA, hw), lsel),
            _const_spec((None, MLA_KV_LORA, hw), lsel),
            _const_spec((None, MLA_KV_LORA, MLA_WIDTH), lsel),
            pl.BlockSpec((TM, HEAD_PAD), lambda b, t: (t, 0)),
            pl.BlockSpec((TM, HEAD_PAD), lambda b, t: (t, 0)),
        ],
        out_specs=[_tok_spec(geom, hw), _tok_spec(geom, hw), _tok_spec(geom, MLA_WIDTH)],
        out_shape=[jax.ShapeDtypeStruct((geom.n, hw), BF16),
                   jax.ShapeDtypeStruct((geom.n, hw), BF16),
                   jax.ShapeDtypeStruct((geom.n, MLA_WIDTH), BF16)],
        compiler_params=_params(("parallel", "parallel")),
        name="mla_proj",
    )(mla, q_norm, kv_norm, wq, wqs, wk, wv, cos_t, sin_t)


HEADS_PER_STEP = 2


def _attn_kernel(q_ref, k_ref, v_ref, o_ref, *, nct, ctx_len, tt):
    t = pl.program_id(2)

    def run(nk):
        outs = []
        for j in range(HEADS_PER_STEP):
            q = q_ref[:, j * HEAD_PAD:(j + 1) * HEAD_PAD]
            k = k_ref[0:nk, j * HEAD_PAD:(j + 1) * HEAD_PAD]
            v = v_ref[0:nk, j * MLA_V:(j + 1) * MLA_V]
            s = lax.dot_general(q, k, (((1,), (1,)), ((), ())), preferred_element_type=F32)
            p = jnp.exp(s - jnp.max(s, -1, keepdims=True))
            den = jnp.sum(p, -1, keepdims=True)
            outs.append(jnp.dot(p.astype(BF16), v, preferred_element_type=F32) / den)
        o_ref[...] = jnp.concatenate(outs, -1)

    pl.when(t < nct)(lambda: run(ctx_len))
    pl.when(t >= nct)(lambda: run(tt))


def _attention(geom, q, k, v):
    qw = HEADS_PER_STEP * HEAD_PAD
    vw = HEADS_PER_STEP * MLA_V
    return pl.pallas_call(
        functools.partial(_attn_kernel, nct=geom.nct, ctx_len=geom.ctx_len, tt=geom.tt),
        grid=(geom.B, MLA_HEADS // HEADS_PER_STEP, geom.nt),
        in_specs=[
            pl.BlockSpec((TM, qw), lambda b, h, t: (b * geom.nt + t, h)),
            pl.BlockSpec((geom.tt, qw), lambda b, h, t: (b, h)),
            pl.BlockSpec((geom.tt, vw), lambda b, h, t: (b, h)),
        ],
        out_specs=pl.BlockSpec((TM, vw), lambda b, h, t: (b * geom.nt + t, h)),
        out_shape=jax.ShapeDtypeStruct((geom.n, MLA_WIDTH), F32),
        compiler_params=_params(("parallel", "parallel", "parallel")),
        name="attn",
    )(q, k, v)


def _shift_prev(x, halo_row):
    rows = lax.broadcasted_iota(jnp.int32, x.shape, 0)
    return jnp.where(rows == 0, halo_row, pltpu.roll(x, 1, 0))


def _shift_next(x, halo_row):
    rows = lax.broadcasted_iota(jnp.int32, x.shape, 0)
    return jnp.where(rows == x.shape[0] - 1, halo_row, pltpu.roll(x, x.shape[0] - 1, 0))


def _feat_kernel(rw_ref, rwp_ref, rwn_ref, cv_ref, cvp_ref, cvn_ref,
                 mu_ref, w0_ref, wup_ref, a0_ref, aup_ref, gup_ref, kk_ref, ka_ref, rk_ref,
                 cw_ref, bsum_ref,
                 r_o, v_o, kk_o, g_o, bonus_o, conv_o, kd_o, b_o, lw_o, *, nct, nt):
    t = pl.program_id(1)
    first = jnp.logical_or(t == 0, t == nct).astype(F32)
    last = jnp.logical_or(t == nct - 1, t == nt - 1).astype(F32)
    keep_prev = 1.0 - first
    keep_next = 1.0 - last
    W = RW_WIDTH

    def gated(ref):
        blk = ref[...]
        return blk[..., 2 * CONV_WIDTH:] * blk[..., :CONV_WIDTH]
    cblk = cv_ref[...]
    up = cblk[:, 2 * CONV_WIDTH:] * cblk[:, :CONV_WIDTH]
    up_p = _shift_prev(up, gated(cvp_ref)[HALO - 1:HALO] * keep_prev)
    up_n = _shift_next(up, gated(cvn_ref)[0:1] * keep_next)
    cw = cw_ref[...]
    conv_o[...] = cblk[:, CONV_WIDTH:2 * CONV_WIDTH] * (up_p * cw[0:1] + up * cw[1:2] + up_n * cw[2:3])

    p = rw_ref[...]
    p_prev = _shift_prev(p, rwp_ref[HALO - 1:HALO, :] * keep_prev)
    p_next = _shift_next(p, rwn_ref[0:1, :] * keep_next)
    p = p + (0.5 * (p_prev + p_next) - p) * mu_ref[...]
    r = p[:, 0:W]
    k = p[:, W:2 * W]
    v = p[:, 2 * W:3 * W]
    wd = p[:, 3 * W:3 * W + 2 * RW_DECAY_LORA]
    ad = p[:, 3 * W + 2 * RW_DECAY_LORA:3 * W + 2 * RW_DECAY_LORA + 2 * RW_ICLR_LORA]
    gd = p[:, 3 * W + 2 * RW_DECAY_LORA + 2 * RW_ICLR_LORA:]

    bsum = bsum_ref[...]
    kkf = k * kk_ref[...]
    kk = kkf * lax.rsqrt(jnp.maximum(_dot_exact_rhs(kkf * kkf, bsum), 1e-24))
    r_o[...] = r
    v_o[...] = v
    kk_o[...] = kk
    g_o[...] = _dot3(_sigmoid(gd), gup_ref[...])
    bonus_o[...] = _dot_exact_rhs(r * k * rk_ref[...], bsum) * v
    for z in range(2):
        wdz = jnp.tanh(wd[:, z * RW_DECAY_LORA:(z + 1) * RW_DECAY_LORA])
        adz = ad[:, z * RW_ICLR_LORA:(z + 1) * RW_ICLR_LORA]
        tz = w0_ref[z:z + 1, :] + _dot3(wdz, wup_ref[z])
        lw_o[z] = _sigmoid(tz) * (-float(np.exp(-0.5)))
        a = _sigmoid(a0_ref[z:z + 1, :] + _dot3(adz, aup_ref[z]))
        kd_o[z] = k * (1.0 + (a - 1.0) * ka_ref[...])
        b_o[z] = kk * a


def _features(geom, layer, rw, conv, prm, bsum):
    nt, n = geom.nt, geom.n
    blocks_per_tile = TM // HALO
    last_halo_block = n // HALO - 1

    def prev_spec(w):
        return pl.BlockSpec((HALO, w), lambda b, t: (jnp.maximum((b * nt + t) * blocks_per_tile - 1, 0), 0))

    def next_spec(w):
        return pl.BlockSpec((HALO, w), lambda b, t: (jnp.minimum((b * nt + t + 1) * blocks_per_tile, last_halo_block), 0))

    lsel2 = lambda b, t: (layer, 0, 0)
    lsel3 = lambda b, t: (layer, 0, 0, 0)
    W = RW_WIDTH
    tok = _tok_spec(geom, W)
    dir_spec = pl.BlockSpec((2, TM, W), lambda b, t: (0, b * nt + t, 0))
    tok_shape = jax.ShapeDtypeStruct((n, W), F32)
    dir_shape = jax.ShapeDtypeStruct((2, n, W), F32)
    return pl.pallas_call(
        functools.partial(_feat_kernel, nct=geom.nct, nt=nt),
        grid=(geom.B, nt),
        in_specs=[
            _tok_spec(geom, RW_IN), prev_spec(RW_IN), next_spec(RW_IN),
            _tok_spec(geom, CONV_SEG), prev_spec(CONV_SEG), next_spec(CONV_SEG),
            pl.BlockSpec((None, 1, RW_IN), lsel2),
            pl.BlockSpec((None, 2, W), lsel2),
            pl.BlockSpec((None, 2, RW_DECAY_LORA, W), lsel3),
            pl.BlockSpec((None, 2, W), lsel2),
            pl.BlockSpec((None, 2, RW_ICLR_LORA, W), lsel3),
            pl.BlockSpec((None, RW_GATE_LORA, W), lsel2),
            pl.BlockSpec((None, 1, W), lsel2),
            pl.BlockSpec((None, 1, W), lsel2),
            pl.BlockSpec((None, 1, W), lsel2),
            pl.BlockSpec((None, 3, CONV_WIDTH), lsel2),
            pl.BlockSpec((W, W), lambda b, t: (0, 0)),
        ],
        out_specs=[tok] * 6 + [dir_spec] * 3,
        out_shape=[tok_shape] * 6 + [dir_shape] * 3,
        compiler_params=_params(("parallel", "parallel")),
        name="feat",
    )(rw, rw, rw, conv, conv, conv,
      prm["rw_mu"], prm["rw_w0"], prm["rw_w_up"], prm["rw_a0"], prm["rw_a_up"], prm["rw_g_up"],
      prm["rw_k_k"], prm["rw_k_a"], prm["rw_r_k"], prm["conv_w"], bsum)


def _tri_inverse(a, eye):
    x = eye - a
    p = a
    steps = int(np.log2(CHUNK)) - 1
    for i in range(steps):
        p = _dot3(p, p)
        x = x + _dot3(x, p)
    return x


def _wkv_prep_kernel(r_ref, v_ref, kk_ref, kd_ref, b_ref, lw_ref, g_o, h_o, rq_o, y0_o):
    z = pl.program_id(0)
    fwd = z == 0
    C = CHUNK
    ri = lax.broadcasted_iota(jnp.int32, (C, C), 0)
    ci = lax.broadcasted_iota(jnp.int32, (C, C), 1)
    strict = (ci - ri) * jnp.where(fwd, 1, -1) < 0
    diag = ri == ci
    incl = jnp.logical_or(strict, diag)
    eye = diag.astype(F32)
    incl_bf = incl.astype(F32).astype(BF16)
    tn = (((0,), (0,)), ((), ()))
    nt_dims = (((1,), (1,)), ((), ()))

    for c in range(TM // C):
        rows = slice(c * C, (c + 1) * C)
        lw = lw_ref[rows, :]
        L = _dot_exact_rhs_lhs(incl_bf, lw)
        Ltot = jnp.where(fwd, L[C - 1:C, :], L[0:1, :])
        eL = jnp.exp(L)
        enL = jnp.exp(-L)
        kk = kk_ref[rows, :]
        kap = kk * jnp.exp(L - lw)
        rt = r_ref[rows, :] * eL
        kt = kd_ref[rows, :] * enL
        bt = b_ref[rows, :] * enL
        etail = jnp.exp(Ltot - L)
        ktp = kd_ref[rows, :] * etail
        btp = b_ref[rows, :] * etail
        ptot = jnp.exp(Ltot)
        vv = v_ref[rows, :]
        for h in range(RW_HEADS):
            hs = slice(h * RW_HEAD, (h + 1) * RW_HEAD)
            lhs = jnp.concatenate([kap[:, hs], rt[:, hs]], axis=0)
            rhs = jnp.concatenate([bt[:, hs], kt[:, hs]], axis=0)
            aa = _dot3(lhs, rhs, nt_dims)
            a_b = jnp.where(strict, aa[:C, :C], 0.0)
            a_k = jnp.where(strict, aa[:C, C:], 0.0)
            a_rb = jnp.where(incl, aa[C:, :C], 0.0)
            a_rk = jnp.where(incl, aa[C:, C:], 0.0)
            tinv = _tri_inverse(a_b, eye)
            vh = vv[:, hs]
            wm = _dot3(tinv, kap[:, hs])
            uv = _dot3(tinv, _dot3(a_k, vh))
            rq_o[rows, hs] = rt[:, hs] - _dot3(a_rb, wm)
            y0_o[rows, hs] = _dot3(a_rk, vh) - _dot3(a_rb, uv)
            btp_h = btp[:, hs]
            g_o[rows, hs] = jnp.where(diag, ptot[:, hs], 0.0) - _dot3(btp_h, wm, tn)
            h_o[rows, hs] = _dot3(ktp[:, hs], vh, tn) - _dot3(btp_h, uv, tn)


def _dot_exact_rhs_lhs(mask_bf16, x):
    x1 = x.astype(BF16)
    r1 = x - x1.astype(F32)
    x2 = r1.astype(BF16)
    x3 = (r1 - x2.astype(F32)).astype(BF16)
    f = lambda u: jnp.dot(mask_bf16, u, preferred_element_type=F32)
    return f(x1) + (f(x2) + f(x3))


def _wkv_prep(geom, r, v, kk, kd, b, lw):
    W = RW_WIDTH
    ntiles = geom.n // TM
    tok = pl.BlockSpec((TM, W), lambda z, i: (i, 0))
    dirs = pl.BlockSpec((None, TM, W), lambda z, i: (z, i, 0))
    shape = jax.ShapeDtypeStruct((2, geom.n, W), F32)
    return pl.pallas_call(
        _wkv_prep_kernel,
        grid=(2, ntiles),
        in_specs=[tok, tok, tok, dirs, dirs, dirs],
        out_specs=[dirs] * 4,
        out_shape=[shape] * 4,
        compiler_params=_params(("parallel", "parallel")),
        name="wkv_prep",
    )(r, v, kk, kd, b, lw)


def _wkv_scan_kernel(g_ref, h_ref, rq_ref, y0_ref, y_o, state):
    z = pl.program_id(0)
    t = pl.program_id(2)
    C = CHUNK
    nch = TM // C

    @pl.when(t == 0)
    def _():
        state[...] = jnp.zeros_like(state)

    def chunk(c, carry):
        cc = jnp.where(z == 0, c, nch - 1 - c)
        rows = pl.ds(pl.multiple_of(cc * C, C), C)
        g = g_ref[rows, :]
        hh = h_ref[rows, :]
        rq = rq_ref[rows, :]
        y0 = y0_ref[rows, :]
        m = state[...]
        ys, ms = [], []
        for h in range(RW_HEADS):
            hs = slice(h * RW_HEAD, (h + 1) * RW_HEAD)
            mh = m[:, hs]
            ys.append(_dot3(rq[:, hs], mh) + y0[:, hs])
            ms.append(_dot3(g[:, hs], mh) + hh[:, hs])
        y_o[rows, :] = jnp.concatenate(ys, -1)
        state[...] = jnp.concatenate(ms, -1)
        return carry

    lax.fori_loop(0, nch, chunk, 0)


def _wkv_scan(geom, g, h, rq, y0):
    W = RW_WIDTH
    nt, nct = geom.nt, geom.nct

    def tile(z, b, t):
        back = jnp.where(t < nct, nct - 1 - t, nt - 1 - (t - nct))
        return (z, b * nt + jnp.where(z == 0, t, back), 0)

    spec = pl.BlockSpec((None, TM, W), tile)
    return pl.pallas_call(
        _wkv_scan_kernel,
        grid=(2, geom.B, nt),
        in_specs=[spec] * 4,
        out_specs=spec,
        out_shape=jax.ShapeDtypeStruct((2, geom.n, W), F32),
        scratch_shapes=[pltpu.VMEM((RW_HEAD, W), F32)],
        compiler_params=_params(("parallel", "parallel", "arbitrary")),
        name="wkv_scan",
    )(g, h, rq, y0)


def _mix_ffn_kernel(x_ref, g1_ref, sh2_ref, sc2_ref, g2_ref,
                    y_ref, gg_ref, bonus_ref, gng_ref, gnb_ref, bavg_ref,
                    att_ref, conv_ref, gate_ref,
                    woa_ref, woc_ref, wor_ref, wout_ref,
                    l1g_ref, l1b_ref, w13_ref, w2_ref, l2g_ref, l2b_ref,
                    o_ref, *, alpha):
    y = y_ref[0] + y_ref[1]
    bavg = bavg_ref[...]
    mu = _dot_exact_rhs(y, bavg) * (1.0 / RW_HEAD)
    yc = y - mu
    var = _dot_exact_rhs(yc * yc, bavg) * (1.0 / RW_HEAD)
    yn = yc * lax.rsqrt(var + RW_GN_EPS) * gng_ref[...] + gnb_ref[...]
    rwo = (yn + bonus_ref[...]) * gg_ref[...]

    gl = gate_ref[...]
    merged = (_sigmoid(gl[:, 0:D_MODEL]) * _bdot(att_ref[...], woa_ref[...])
              + _sigmoid(gl[:, D_MODEL:2 * D_MODEL]) * _bdot(conv_ref[...], woc_ref[...])
              + _sigmoid(gl[:, 2 * D_MODEL:]) * _bdot(rwo, wor_ref[...]))
    o = _bdot(merged, wout_ref[...])
    x = x_ref[...]
    x1 = _layer_norm(alpha * x + g1_ref[...] * o, l1g_ref[...], l1b_ref[...])

    hmod = x1 * (1.0 + sc2_ref[...]) + sh2_ref[...]
    ug = _bdot(hmod, w13_ref[...])
    u = ug[:, :D_FF]
    f = _bdot(u * _sigmoid(u) * ug[:, D_FF:], w2_ref[...])
    o_ref[...] = _layer_norm(alpha * x1 + g2_ref[...] * f, l2g_ref[...], l2b_ref[...])


def _mix_ffn(geom, layer, alpha, x, modv, y, g, bonus, att, conv, gate, prm, bavg):
    W = RW_WIDTH
    nt = geom.nt
    lsel = lambda b, t: (layer, 0, 0)
    vec = lambda w: pl.BlockSpec((None, 1, w), lsel)
    wspec = lambda r, c: _const_spec((None, r, c), lsel)
    return pl.pallas_call(
        functools.partial(_mix_ffn_kernel, alpha=alpha),
        grid=(geom.B, nt),
        in_specs=[
            _tok_spec(geom, D_MODEL),
            _mod_spec(geom, layer, 2), _mod_spec(geom, layer, 3), _mod_spec(geom, layer, 4),
            _mod_spec(geom, layer, 5),
            pl.BlockSpec((2, TM, W), lambda b, t: (0, b * nt + t, 0)),
            _tok_spec(geom, W), _tok_spec(geom, W), vec(W), vec(W),
            pl.BlockSpec((W, W), lambda b, t: (0, 0)),
            _tok_spec(geom, MLA_WIDTH), _tok_spec(geom, CONV_WIDTH), _tok_spec(geom, GATE_SEG),
            wspec(MLA_WIDTH, D_MODEL), wspec(CONV_WIDTH, D_MODEL), wspec(W, D_MODEL),
            wspec(D_MODEL, D_MODEL),
            vec(D_MODEL), vec(D_MODEL),
            wspec(D_MODEL, 2 * D_FF), wspec(D_FF, D_MODEL),
            vec(D_MODEL), vec(D_MODEL),
        ],
        out_specs=_tok_spec(geom, D_MODEL),
        out_shape=jax.ShapeDtypeStruct((geom.n, D_MODEL), F32),
        compiler_params=_params(("parallel", "parallel")),
        name="mix_ffn",
    )(x, modv, modv, modv, modv, y, g, bonus, prm["rw_gn_g"], prm["rw_gn_b"], bavg,
      att, conv, gate,
      prm["w_o_attn"], prm["w_o_conv"], prm["w_o_rwkv"], prm["w_out"],
      prm["ln1_g"], prm["ln1_b"], prm["ffn_w13"], prm["ffn_w2"], prm["ln2_g"], prm["ln2_b"])


_ROPE_SWAP = np.concatenate([np.arange(8, 16), np.arange(0, 8), np.arange(24, 32), np.arange(16, 24)])


def _rope_tables(geom):
    pos = jnp.arange(geom.seq)
    row = (pos // GRID_W).astype(F32)
    col = (pos % GRID_W).astype(F32)
    axis_dim = MLA_ROPE // 2
    inv = ROPE_BASE ** (-jnp.arange(0, axis_dim, 2, dtype=F32) / axis_dim)
    ar, ac = row[:, None] * inv, col[:, None] * inv
    cr, sr, cc, sc = jnp.cos(ar), jnp.sin(ar), jnp.cos(ac), jnp.sin(ac)
    cos32 = jnp.concatenate([cr, cr, cc, cc], -1)
    sin32 = jnp.concatenate([-sr, sr, -sc, sc], -1)
    ones = jnp.ones((geom.seq, MLA_NOPE), F32)
    zpad = jnp.zeros((geom.seq, HEAD_PAD - MLA_NOPE - MLA_ROPE), F32)
    cos_l = jnp.concatenate([ones, cos32, zpad], -1)
    sin_l = jnp.concatenate([jnp.zeros_like(ones), sin32, zpad], -1)
    cos_c = jnp.ones((geom.ctx_len, HEAD_PAD), F32)
    sin_c = jnp.zeros((geom.ctx_len, HEAD_PAD), F32)
    return jnp.concatenate([cos_c, cos_l], 0), jnp.concatenate([sin_c, sin_l], 0)


def _layout_weights(w_in, w_uq, w_ukv):
    L = w_in.shape[0]
    o_q, o_kv, o_kr = MLA_Q_LORA, MLA_Q_LORA + MLA_KV_LORA, MLA_Q_LORA + MLA_KV_LORA + MLA_ROPE
    krope = w_in[:, :, o_kv:o_kr]
    zl = jnp.zeros((L, D_MODEL, MLA_NOPE), F32)
    zr = jnp.zeros((L, D_MODEL, HEAD_PAD - MLA_NOPE - MLA_ROPE), F32)
    w_in_p = jnp.concatenate(
        [w_in[:, :, :o_kv], zl, krope, zr, zl, krope[:, :, _ROPE_SWAP], zr, w_in[:, :, o_kr:]], -1).astype(BF16)

    wq = w_uq.reshape(L, MLA_Q_LORA, MLA_HEADS, MLA_NOPE + MLA_ROPE)
    q_rope = wq[..., MLA_NOPE:]
    zq = jnp.zeros((L, MLA_Q_LORA, MLA_HEADS, HEAD_PAD - MLA_NOPE - MLA_ROPE), F32)
    zn = jnp.zeros((L, MLA_Q_LORA, MLA_HEADS, MLA_NOPE), F32)
    wq_p = jnp.concatenate([wq, zq], -1).reshape(L, MLA_Q_LORA, -1).astype(BF16)
    wqs_p = jnp.concatenate([zn, q_rope[..., _ROPE_SWAP], zq], -1).reshape(L, MLA_Q_LORA, -1).astype(BF16)

    wkv = w_ukv.reshape(L, MLA_KV_LORA, MLA_HEADS, MLA_NOPE + MLA_V)
    zk = jnp.zeros((L, MLA_KV_LORA, MLA_HEADS, HEAD_PAD - MLA_NOPE), F32)
    wk_p = jnp.concatenate([wkv[..., :MLA_NOPE], zk], -1).reshape(L, MLA_KV_LORA, -1).astype(BF16)
    wv_p = wkv[..., MLA_NOPE:].reshape(L, MLA_KV_LORA, -1).astype(BF16)
    return w_in_p, wq_p, wqs_p, wk_p, wv_p


def _head_block_ones():
    idx = np.arange(RW_WIDTH) // RW_HEAD
    return jnp.asarray((idx[:, None] == idx[None, :]).astype(np.float32), dtype=BF16)


def kernel(x, c, ctx, c_ctx, mod_w, mod_b, w_in, q_norm, w_uq, kv_norm, w_ukv, w_o_attn, conv_w, w_o_conv,
           rw_mu, rw_w0, rw_w_up, rw_a0, rw_a_up, rw_g_up, rw_k_k, rw_k_a, rw_r_k, rw_gn_g, rw_gn_b,
           w_o_rwkv, w_out, ln1_g, ln1_b, ffn_w13, ffn_w2, ln2_g, ln2_b):
    B, seq, _ = x.shape
    ctx_len = ctx.shape[1]
    L = mod_w.shape[0]
    geom = _Geom(B, ctx_len, seq)
    alpha = (2.0 * L) ** 0.25

    rows = -(-(B + 1) // HALO) * HALO
    c_all = jnp.concatenate([c, c_ctx[None, :], jnp.zeros((rows - B - 1, D_MODEL), F32)], 0)
    modv = _mod_vectors(c_all, mod_w, mod_b).reshape(L, rows, 1, -1)

    w_in_p, wq_p, wqs_p, wk_p, wv_p = _layout_weights(w_in, w_uq, w_ukv)
    cos_t, sin_t = _rope_tables(geom)
    bsum = _head_block_ones()
    vec3 = lambda a: a.reshape(L, 1, -1)
    prm = dict(
        rw_mu=vec3(rw_mu), rw_w0=rw_w0, rw_w_up=rw_w_up, rw_a0=rw_a0, rw_a_up=rw_a_up, rw_g_up=rw_g_up,
        rw_k_k=vec3(rw_k_k), rw_k_a=vec3(rw_k_a), rw_r_k=vec3(rw_r_k), conv_w=conv_w,
        rw_gn_g=vec3(rw_gn_g), rw_gn_b=vec3(rw_gn_b),
        w_o_attn=w_o_attn.astype(BF16), w_o_conv=w_o_conv.astype(BF16), w_o_rwkv=w_o_rwkv.astype(BF16),
        w_out=w_out.astype(BF16), ln1_g=vec3(ln1_g), ln1_b=vec3(ln1_b),
        ffn_w13=ffn_w13.astype(BF16), ffn_w2=ffn_w2.astype(BF16), ln2_g=vec3(ln2_g), ln2_b=vec3(ln2_b),
    )
    qn, kvn = vec3(q_norm), vec3(kv_norm)

    xs = jnp.concatenate([ctx, x], axis=1).reshape(geom.n, D_MODEL)
    for l in range(L):
        mla, conv, rw, gate = _in_proj(geom, l, xs, modv, w_in_p)
        q, k, v = _mla_proj(geom, l, mla, qn, kvn, wq_p, wqs_p, wk_p, wv_p, cos_t, sin_t)
        att = _attention(geom, q, k, v)
        r, vv, kk, g, bonus, cv, kd, bb, lw = _features(geom, l, rw, conv, prm, bsum)
        gm, hm, rq, y0 = _wkv_prep(geom, r, vv, kk, kd, bb, lw)
        y = _wkv_scan(geom, gm, hm, rq, y0)
        xs = _mix_ffn(geom, l, alpha, xs, modv, y, g, bonus, att, cv, gate, prm, bsum)
    return xs.reshape(B, geom.tt, D_MODEL)[:, ctx_len:, :]
```

```python
import functools

import numpy as np
import jax
import jax.numpy as jnp
from jax import lax
from jax.experimental import pallas as pl
from jax.experimental.pallas import tpu as pltpu

F32 = jnp.float32
BF16 = jnp.bfloat16

D_MODEL = 1024
GRID_W = 64
MLA_HEADS = 8
MLA_Q_LORA = 384
MLA_KV_LORA = 256
MLA_NOPE = 64
MLA_ROPE = 32
MLA_V = 64
MLA_WIDTH = MLA_HEADS * MLA_V
ROPE_BASE = 10000.0
CONV_WIDTH = 512
RW_HEADS = 8
RW_HEAD = 64
RW_WIDTH = RW_HEADS * RW_HEAD
RW_DECAY_LORA = 64
RW_ICLR_LORA = 64
RW_GATE_LORA = 128
RW_GN_EPS = 64e-5
RW_IN = 3 * RW_WIDTH + 2 * RW_DECAY_LORA + 2 * RW_ICLR_LORA + RW_GATE_LORA
N_BRANCH = 3
D_FF = 2816
LN_EPS = 1e-5
RMS_EPS = 1e-6

LANES = 128
HEAD_PAD = LANES
TM = 256
CHUNK = 64
PREP_INTERLEAVE = 4
HALO = 8
VMEM_LIMIT = 56 * 1024 * 1024

MLA_SEG = MLA_Q_LORA + MLA_KV_LORA + 2 * HEAD_PAD
CONV_SEG = 3 * CONV_WIDTH
GATE_SEG = N_BRANCH * D_MODEL
IN_SEGS = (MLA_SEG, CONV_SEG, RW_IN, GATE_SEG)
IN_OFFS = tuple(int(v) for v in np.cumsum((0,) + IN_SEGS))


def _bdot(a, b):
    return jnp.dot(a.astype(BF16), b.astype(BF16), preferred_element_type=F32)


def _split2(a):
    hi = a.astype(BF16)
    lo = (a - hi.astype(F32)).astype(BF16)
    return hi, lo


def _dot3(a, b, dims=None):
    ah, al = _split2(a)
    bh, bl = _split2(b)
    if dims is None:
        f = lambda u, v: jnp.dot(u, v, preferred_element_type=F32)
    else:
        f = lambda u, v: lax.dot_general(u, v, dims, preferred_element_type=F32)
    return f(ah, bh) + (f(ah, bl) + f(al, bh))


def _dot1(a, b, dims=None):
    a, b = a.astype(BF16), b.astype(BF16)
    if dims is None:
        return jnp.dot(a, b, preferred_element_type=F32)
    return lax.dot_general(a, b, dims, preferred_element_type=F32)


_dot_wkv = _dot1
_dot_inv = _dot1


def _dot_exact_rhs(a, b_bf16):
    a1 = a.astype(BF16)
    r1 = a - a1.astype(F32)
    a2 = r1.astype(BF16)
    a3 = (r1 - a2.astype(F32)).astype(BF16)
    f = lambda u: jnp.dot(u, b_bf16, preferred_element_type=F32)
    return f(a1) + (f(a2) + f(a3))


def _sigmoid(x):
    return 1.0 / (1.0 + jnp.exp(-x))


def _layer_norm(x, g, b):
    mu = jnp.mean(x, -1, keepdims=True)
    xc = x - mu
    var = jnp.mean(xc * xc, -1, keepdims=True)
    return xc * lax.rsqrt(var + LN_EPS) * g + b


def _const_spec(shape, index_map):
    return pl.BlockSpec(shape, index_map, pipeline_mode=pl.Buffered(1))


def _params(sem):
    return pltpu.CompilerParams(dimension_semantics=sem, vmem_limit_bytes=VMEM_LIMIT)


def _mod_kernel(c_ref, w_ref, b_ref, o_ref):
    c = c_ref[...]
    s = c * _sigmoid(c)
    o_ref[...] = _dot3(s, w_ref[...]) + b_ref[...]


def _mod_vectors(c_all, mod_w, mod_b):
    L = mod_w.shape[0]
    R = c_all.shape[0]
    n = mod_w.shape[2] // D_MODEL
    return pl.pallas_call(
        _mod_kernel,
        grid=(L, n),
        in_specs=[
            pl.BlockSpec((R, D_MODEL), lambda l, j: (0, 0)),
            pl.BlockSpec((None, D_MODEL, D_MODEL), lambda l, j: (l, 0, j)),
            pl.BlockSpec((None, 1, D_MODEL), lambda l, j: (l, 0, j)),
        ],
        out_specs=pl.BlockSpec((None, R, D_MODEL), lambda l, j: (l, 0, j)),
        out_shape=jax.ShapeDtypeStruct((L, R, n * D_MODEL), F32),
        compiler_params=_params(("parallel", "parallel")),
        name="mod_vectors",
    )(c_all, mod_w, mod_b.reshape(L, 1, -1))


class _Geom:
    def __init__(self, B, ctx_len, seq):
        assert ctx_len % TM == 0 and seq % TM == 0 and seq % GRID_W == 0
        self.B, self.ctx_len, self.seq = B, ctx_len, seq
        self.tt = ctx_len + seq
        self.nct = ctx_len // TM
        self.nt = self.tt // TM
        self.n = B * self.tt

    def mod_row(self, b, t):
        return jnp.where(t < self.nct, self.B, b)


def _mod_spec(geom, layer, j):
    return pl.BlockSpec((None, None, 1, D_MODEL), lambda b, t: (layer, geom.mod_row(b, t), 0, j))


def _tok_spec(geom, width):
    return pl.BlockSpec((TM, width), lambda b, t: (b * geom.nt + t, 0))


def _in_proj_kernel(x_ref, sh_ref, sc_ref, w_ref, mla_ref, conv_ref, rw_ref, gate_ref):
    h = (x_ref[...] * (1.0 + sc_ref[...]) + sh_ref[...]).astype(BF16)
    for o_ref, lo, hi in zip((mla_ref, conv_ref, rw_ref, gate_ref), IN_OFFS[:-1], IN_OFFS[1:]):
        o_ref[...] = jnp.dot(h, w_ref[:, lo:hi], preferred_element_type=F32)


def _in_proj(geom, layer, x, modv, w_in_p):
    return pl.pallas_call(
        _in_proj_kernel,
        grid=(geom.B, geom.nt),
        in_specs=[
            _tok_spec(geom, D_MODEL),
            _mod_spec(geom, layer, 0),
            _mod_spec(geom, layer, 1),
            _const_spec((None, D_MODEL, IN_OFFS[-1]), lambda b, t: (layer, 0, 0)),
        ],
        out_specs=[_tok_spec(geom, w) for w in IN_SEGS],
        out_shape=[jax.ShapeDtypeStruct((geom.n, w), F32) for w in IN_SEGS],
        compiler_params=_params(("parallel", "parallel")),
        name="in_proj",
    )(x, modv, modv, w_in_p)


def _mla_kernel(m_ref, qn_ref, kvn_ref, wq_ref, wqs_ref, wk_ref, wv_ref, cos_ref, sin_ref,
                q_ref, k_ref, v_ref):
    m = m_ref[...]
    cq = m[:, :MLA_Q_LORA]
    ckv = m[:, MLA_Q_LORA:MLA_Q_LORA + MLA_KV_LORA]
    kslab = m[:, MLA_Q_LORA + MLA_KV_LORA:MLA_Q_LORA + MLA_KV_LORA + HEAD_PAD]
    kslab_sw = m[:, MLA_Q_LORA + MLA_KV_LORA + HEAD_PAD:]
    cqn = (cq * lax.rsqrt(jnp.mean(cq * cq, -1, keepdims=True) + RMS_EPS) * qn_ref[...]).astype(BF16)
    ckvn = (ckv * lax.rsqrt(jnp.mean(ckv * ckv, -1, keepdims=True) + RMS_EPS) * kvn_ref[...]).astype(BF16)
    cos_t = cos_ref[...]
    sin_t = sin_ref[...]
    qa = jnp.dot(cqn, wq_ref[...], preferred_element_type=F32)
    qb = jnp.dot(cqn, wqs_ref[...], preferred_element_type=F32)
    kn = jnp.dot(ckvn, wk_ref[...], preferred_element_type=F32)
    kr = kslab * cos_t + kslab_sw * sin_t
    scale = (MLA_NOPE + MLA_ROPE) ** -0.5
    for h in range(MLA_HEADS):
        sl = slice(h * HEAD_PAD, (h + 1) * HEAD_PAD)
        q_ref[:, sl] = ((qa[:, sl] * cos_t + qb[:, sl] * sin_t) * scale).astype(BF16)
        k_ref[:, sl] = (kn[:, sl] + kr).astype(BF16)
    v_ref[...] = jnp.dot(ckvn, wv_ref[...], preferred_element_type=F32).astype(BF16)


def _mla_proj(geom, layer, mla, q_norm, kv_norm, wq, wqs, wk, wv, cos_t, sin_t):
    hw = MLA_HEADS * HEAD_PAD
    lsel = lambda b, t: (layer, 0, 0)
    return pl.pallas_call(
        _mla_kernel,
        grid=(geom.B, geom.nt),
        in_specs=[
            _tok_spec(geom, MLA_SEG),
            pl.BlockSpec((None, 1, MLA_Q_LORA), lsel),
            pl.BlockSpec((None, 1, MLA_KV_LORA), lsel),
            _const_spec((None, MLA_Q_LORA, hw), lsel),
            _const_spec((None, MLA_Q_LORA, hw), lsel),
            _const_spec((None, MLA_KV_LORA, hw), lsel),
            _const_spec((None, MLA_KV_LORA, MLA_WIDTH), lsel),
            pl.BlockSpec((TM, HEAD_PAD), lambda b, t: (t, 0)),
            pl.BlockSpec((TM, HEAD_PAD), lambda b, t: (t, 0)),
        ],
        out_specs=[_tok_spec(geom, hw), _tok_spec(geom, hw), _tok_spec(geom, MLA_WIDTH)],
        out_shape=[jax.ShapeDtypeStruct((geom.n, hw), BF16),
                   jax.ShapeDtypeStruct((geom.n, hw), BF16),
                   jax.ShapeDtypeStruct((geom.n, MLA_WIDTH), BF16)],
        compiler_params=_params(("parallel", "parallel")),
        name="mla_proj",
    )(mla, q_norm, kv_norm, wq, wqs, wk, wv, cos_t, sin_t)


HEADS_PER_STEP = 2


def _attn_kernel(q_ref, k_ref, v_ref, o_ref, *, nct, ctx_len, tt):
    t = pl.program_id(2)

    def run(nk):
        outs = []
        for j in range(HEADS_PER_STEP):
            q = q_ref[:, j * HEAD_PAD:(j + 1) * HEAD_PAD]
            k = k_ref[0:nk, j * HEAD_PAD:(j + 1) * HEAD_PAD]
            v = v_ref[0:nk, j * MLA_V:(j + 1) * MLA_V]
            s = lax.dot_general(q, k, (((1,), (1,)), ((), ())), preferred_element_type=F32)
            p = jnp.exp(s - jnp.max(s, -1, keepdims=True))
            den = jnp.sum(p, -1, keepdims=True)
            outs.append(jnp.dot(p.astype(BF16), v, preferred_element_type=F32) / den)
        o_ref[...] = jnp.concatenate(outs, -1)

    pl.when(t < nct)(lambda: run(ctx_len))
    pl.when(t >= nct)(lambda: run(tt))


def _attention(geom, q, k, v):
    qw = HEADS_PER_STEP * HEAD_PAD
    vw = HEADS_PER_STEP * MLA_V
    return pl.pallas_call(
        functools.partial(_attn_kernel, nct=geom.nct, ctx_len=geom.ctx_len, tt=geom.tt),
        grid=(geom.B, MLA_HEADS // HEADS_PER_STEP, geom.nt),
        in_specs=[
            pl.BlockSpec((TM, qw), lambda b, h, t: (b * geom.nt + t, h)),
            pl.BlockSpec((geom.tt, qw), lambda b, h, t: (b, h)),
            pl.BlockSpec((geom.tt, vw), lambda b, h, t: (b, h)),
        ],
        out_specs=pl.BlockSpec((TM, vw), lambda b, h, t: (b * geom.nt + t, h)),
        out_shape=jax.ShapeDtypeStruct((geom.n, MLA_WIDTH), F32),
        compiler_params=_params(("parallel", "parallel", "parallel")),
        name="attn",
    )(q, k, v)


def _shift_prev(x, halo_row):
    rows = lax.broadcasted_iota(jnp.int32, x.shape, 0)
    return jnp.where(rows == 0, halo_row, pltpu.roll(x, 1, 0))


def _shift_next(x, halo_row):
    rows = lax.broadcasted_iota(jnp.int32, x.shape, 0)
    return jnp.where(rows == x.shape[0] - 1, halo_row, pltpu.roll(x, x.shape[0] - 1, 0))


def _feat_kernel(rw_ref, rwp_ref, rwn_ref, cv_ref, cvp_ref, cvn_ref,
                 mu_ref, w0_ref, wup_ref, a0_ref, aup_ref, gup_ref, kk_ref, ka_ref, rk_ref,
                 cw_ref, bsum_ref,
                 r_o, v_o, kk_o, g_o, bonus_o, conv_o, kd_o, b_o, lw_o, *, nct, nt):
    t = pl.program_id(1)
    first = jnp.logical_or(t == 0, t == nct).astype(F32)
    last = jnp.logical_or(t == nct - 1, t == nt - 1).astype(F32)
    keep_prev = 1.0 - first
    keep_next = 1.0 - last
    W = RW_WIDTH

    def gated(ref):
        blk = ref[...]
        return blk[..., 2 * CONV_WIDTH:] * blk[..., :CONV_WIDTH]
    cblk = cv_ref[...]
    up = cblk[:, 2 * CONV_WIDTH:] * cblk[:, :CONV_WIDTH]
    up_p = _shift_prev(up, gated(cvp_ref)[HALO - 1:HALO] * keep_prev)
    up_n = _shift_next(up, gated(cvn_ref)[0:1] * keep_next)
    cw = cw_ref[...]
    conv_o[...] = cblk[:, CONV_WIDTH:2 * CONV_WIDTH] * (up_p * cw[0:1] + up * cw[1:2] + up_n * cw[2:3])

    p = rw_ref[...]
    p_prev = _shift_prev(p, rwp_ref[HALO - 1:HALO, :] * keep_prev)
    p_next = _shift_next(p, rwn_ref[0:1, :] * keep_next)
    p = p + (0.5 * (p_prev + p_next) - p) * mu_ref[...]
    r = p[:, 0:W]
    k = p[:, W:2 * W]
    v = p[:, 2 * W:3 * W]
    wd = p[:, 3 * W:3 * W + 2 * RW_DECAY_LORA]
    ad = p[:, 3 * W + 2 * RW_DECAY_LORA:3 * W + 2 * RW_DECAY_LORA + 2 * RW_ICLR_LORA]
    gd = p[:, 3 * W + 2 * RW_DECAY_LORA + 2 * RW_ICLR_LORA:]

    bsum = bsum_ref[...]
    kkf = k * kk_ref[...]
    kk = kkf * lax.rsqrt(jnp.maximum(_dot_exact_rhs(kkf * kkf, bsum), 1e-24))
    r_o[...] = r
    v_o[...] = v
    kk_o[...] = kk
    g_o[...] = _dot3(_sigmoid(gd), gup_ref[...])
    bonus_o[...] = _dot_exact_rhs(r * k * rk_ref[...], bsum) * v
    for z in range(2):
        wdz = jnp.tanh(wd[:, z * RW_DECAY_LORA:(z + 1) * RW_DECAY_LORA])
        adz = ad[:, z * RW_ICLR_LORA:(z + 1) * RW_ICLR_LORA]
        tz = w0_ref[z:z + 1, :] + _dot3(wdz, wup_ref[z])
        lw_o[z] = _sigmoid(tz) * (-float(np.exp(-0.5)))
        a = _sigmoid(a0_ref[z:z + 1, :] + _dot3(adz, aup_ref[z]))
        kd_o[z] = k * (1.0 + (a - 1.0) * ka_ref[...])
        b_o[z] = kk * a


def _features(geom, layer, rw, conv, prm, bsum):
    nt, n = geom.nt, geom.n
    blocks_per_tile = TM // HALO
    last_halo_block = n // HALO - 1

    def prev_spec(w):
        return pl.BlockSpec((HALO, w), lambda b, t: (jnp.maximum((b * nt + t) * blocks_per_tile - 1, 0), 0))

    def next_spec(w):
        return pl.BlockSpec((HALO, w), lambda b, t: (jnp.minimum((b * nt + t + 1) * blocks_per_tile, last_halo_block), 0))

    lsel2 = lambda b, t: (layer, 0, 0)
    lsel3 = lambda b, t: (layer, 0, 0, 0)
    W = RW_WIDTH
    tok = _tok_spec(geom, W)
    dir_spec = pl.BlockSpec((2, TM, W), lambda b, t: (0, b * nt + t, 0))
    tok_shape = jax.ShapeDtypeStruct((n, W), F32)
    dir_shape = jax.ShapeDtypeStruct((2, n, W), F32)
    return pl.pallas_call(
        functools.partial(_feat_kernel, nct=geom.nct, nt=nt),
        grid=(geom.B, nt),
        in_specs=[
            _tok_spec(geom, RW_IN), prev_spec(RW_IN), next_spec(RW_IN),
            _tok_spec(geom, CONV_SEG), prev_spec(CONV_SEG), next_spec(CONV_SEG),
            pl.BlockSpec((None, 1, RW_IN), lsel2),
            pl.BlockSpec((None, 2, W), lsel2),
            pl.BlockSpec((None, 2, RW_DECAY_LORA, W), lsel3),
            pl.BlockSpec((None, 2, W), lsel2),
            pl.BlockSpec((None, 2, RW_ICLR_LORA, W), lsel3),
            pl.BlockSpec((None, RW_GATE_LORA, W), lsel2),
            pl.BlockSpec((None, 1, W), lsel2),
            pl.BlockSpec((None, 1, W), lsel2),
            pl.BlockSpec((None, 1, W), lsel2),
            pl.BlockSpec((None, 3, CONV_WIDTH), lsel2),
            pl.BlockSpec((W, W), lambda b, t: (0, 0)),
        ],
        out_specs=[tok] * 6 + [dir_spec] * 3,
        out_shape=[tok_shape] * 6 + [dir_shape] * 3,
        compiler_params=_params(("parallel", "parallel")),
        name="feat",
    )(rw, rw, rw, conv, conv, conv,
      prm["rw_mu"], prm["rw_w0"], prm["rw_w_up"], prm["rw_a0"], prm["rw_a_up"], prm["rw_g_up"],
      prm["rw_k_k"], prm["rw_k_a"], prm["rw_r_k"], prm["conv_w"], bsum)


def _tri_inverse_all(a_list, eye):
    xs = [eye - a for a in a_list]
    ps = list(a_list)
    for _ in range(int(np.log2(CHUNK)) - 1):
        ps = [_dot_inv(p, p) for p in ps]
        xs = [x + _dot_inv(x, p) for x, p in zip(xs, ps)]
    return xs


def _wkv_prep_kernel(r_ref, v_ref, kk_ref, kd_ref, b_ref, lw_ref, g_o, h_o, rq_o, y0_o):
    z = pl.program_id(0)
    fwd = z == 0
    C = CHUNK
    ri = lax.broadcasted_iota(jnp.int32, (C, C), 0)
    ci = lax.broadcasted_iota(jnp.int32, (C, C), 1)
    strict = (ci - ri) * jnp.where(fwd, 1, -1) < 0
    diag = ri == ci
    incl = jnp.logical_or(strict, diag)
    eye = diag.astype(F32)
    incl_bf = incl.astype(F32).astype(BF16)
    nt_dims = (((1,), (1,)), ((), ()))

    def prologue(c):
        rows = slice(c * C, (c + 1) * C)
        lw = lw_ref[rows, :]
        L = _dot_exact_rhs_lhs(incl_bf, lw)
        Ltot = jnp.where(fwd, L[C - 1:C, :], L[0:1, :])
        enL = jnp.exp(-L)
        etail = jnp.exp(Ltot - L)
        kd = kd_ref[rows, :]
        bb = b_ref[rows, :]
        return dict(
            rows=rows,
            kap=(kk_ref[rows, :] * jnp.exp(L - lw)).astype(BF16),
            rt=r_ref[rows, :] * jnp.exp(L),
            kt=(kd * enL).astype(BF16),
            bt=(bb * enL).astype(BF16),
            ktp_t=jnp.transpose(kd * etail).astype(BF16),
            btp_t=jnp.transpose(bb * etail).astype(BF16),
            ptot=jnp.exp(Ltot),
            vv=v_ref[rows, :].astype(BF16))

    heads = [slice(h * RW_HEAD, (h + 1) * RW_HEAD) for h in range(RW_HEADS)]
    for c0 in range(0, TM // C, PREP_INTERLEAVE):
        chunks = [prologue(c) for c in range(c0, c0 + PREP_INTERLEAVE)]
        items = [(ch, hs) for ch in chunks for hs in heads]
        aa = [_dot_wkv(jnp.concatenate([ch["kap"][:, hs], ch["rt"][:, hs].astype(BF16)], axis=0),
                       jnp.concatenate([ch["bt"][:, hs], ch["kt"][:, hs]], axis=0), nt_dims)
              for ch, hs in items]
        a_b = [jnp.where(strict, m[:C, :C], 0.0) for m in aa]
        a_kk = [jnp.concatenate([jnp.where(strict, m[:C, C:], 0.0), jnp.where(incl, m[C:, C:], 0.0)], axis=0)
                for m in aa]
        a_rb = [jnp.where(incl, m[C:, :C], 0.0) for m in aa]
        tinv = _tri_inverse_all(a_b, eye)
        avv = [_dot_wkv(m, ch["vv"][:, hs]) for m, (ch, hs) in zip(a_kk, items)]
        wu = [_dot_wkv(t, jnp.concatenate([ch["kap"][:, hs], av[:C].astype(BF16)], axis=1))
              for t, av, (ch, hs) in zip(tinv, avv, items)]
        rbwu = [_dot_wkv(m, x) for m, x in zip(a_rb, wu)]
        bwu = [_dot_wkv(ch["btp_t"][hs, :], x) for x, (ch, hs) in zip(wu, items)]
        kv = [_dot_wkv(ch["ktp_t"][hs, :], ch["vv"][:, hs]) for ch, hs in items]
        nh = RW_HEADS
        for i, ch in enumerate(chunks):
            sel = slice(i * nh, (i + 1) * nh)
            rows = ch["rows"]
            rq_o[rows, :] = (ch["rt"] - jnp.concatenate([m[:, :RW_HEAD] for m in rbwu[sel]], axis=1)).astype(BF16)
            y0_o[rows, :] = jnp.concatenate([av[C:] - m[:, RW_HEAD:] for av, m in zip(avv[sel], rbwu[sel])], axis=1)
            g_o[rows, :] = jnp.concatenate([jnp.where(diag, ch["ptot"][:, hs], 0.0) - m[:, :RW_HEAD]
                                            for m, hs in zip(bwu[sel], heads)], axis=1).astype(BF16)
            h_o[rows, :] = jnp.concatenate([m1 - m2[:, RW_HEAD:] for m1, m2 in zip(kv[sel], bwu[sel])], axis=1)


def _dot_exact_rhs_lhs(mask_bf16, x):
    x1 = x.astype(BF16)
    r1 = x - x1.astype(F32)
    x2 = r1.astype(BF16)
    x3 = (r1 - x2.astype(F32)).astype(BF16)
    f = lambda u: jnp.dot(mask_bf16, u, preferred_element_type=F32)
    return f(x1) + (f(x2) + f(x3))


def _wkv_prep(geom, r, v, kk, kd, b, lw):
    W = RW_WIDTH
    ntiles = geom.n // TM
    tok = pl.BlockSpec((TM, W), lambda z, i: (i, 0))
    dirs = pl.BlockSpec((None, TM, W), lambda z, i: (z, i, 0))
    shapes = [jax.ShapeDtypeStruct((2, geom.n, W), dt) for dt in (BF16, F32, BF16, F32)]
    return pl.pallas_call(
        _wkv_prep_kernel,
        grid=(2, ntiles),
        in_specs=[tok, tok, tok, dirs, dirs, dirs],
        out_specs=[dirs] * 4,
        out_shape=shapes,
        compiler_params=_params(("parallel", "parallel")),
        name="wkv_prep",
    )(r, v, kk, kd, b, lw)


def _wkv_scan_kernel(gf_ref, hf_ref, rqf_ref, y0f_ref, gb_ref, hb_ref, rqb_ref, y0b_ref,
                     yf_o, yb_o, state):
    C = CHUNK
    nch = TM // C

    @pl.when(pl.program_id(1) == 0)
    def _():
        state[...] = jnp.zeros_like(state)

    heads = [slice(h * RW_HEAD, (h + 1) * RW_HEAD) for h in range(RW_HEADS)]
    dirs = ((gf_ref, hf_ref, rqf_ref, y0f_ref, yf_o), (gb_ref, hb_ref, rqb_ref, y0b_ref, yb_o))
    m = [state[0], state[1]]
    for c in range(nch):
        for z, (g_ref, h_ref, rq_ref, y0_ref, y_o) in enumerate(dirs):
            cc = c if z == 0 else nch - 1 - c
            rows = slice(cc * C, (cc + 1) * C)
            g = g_ref[rows, :]
            rq = rq_ref[rows, :]
            mb = m[z].astype(BF16)
            ys = [jnp.dot(rq[:, hs], mb[:, hs], preferred_element_type=F32) for hs in heads]
            ms = [jnp.dot(g[:, hs], mb[:, hs], preferred_element_type=F32) for hs in heads]
            y_o[rows, :] = jnp.concatenate(ys, -1) + y0_ref[rows, :]
            m[z] = jnp.concatenate(ms, -1) + h_ref[rows, :]
    state[0] = m[0]
    state[1] = m[1]


def _wkv_scan(geom, g, h, rq, y0):
    W = RW_WIDTH
    nt, nct = geom.nt, geom.nct

    def fwd_tile(b, t):
        return (0, b * nt + t, 0)

    def bwd_tile(b, t):
        return (1, b * nt + jnp.where(t < nct, nct - 1 - t, nt - 1 - (t - nct)), 0)

    fspec = pl.BlockSpec((None, TM, W), fwd_tile)
    bspec = pl.BlockSpec((None, TM, W), bwd_tile)
    return pl.pallas_call(
        _wkv_scan_kernel,
        grid=(geom.B, nt),
        in_specs=[fspec] * 4 + [bspec] * 4,
        out_specs=[pl.BlockSpec((TM, W), lambda b, t: fwd_tile(b, t)[1:]),
                   pl.BlockSpec((TM, W), lambda b, t: bwd_tile(b, t)[1:])],
        out_shape=[jax.ShapeDtypeStruct((geom.n, W), F32)] * 2,
        scratch_shapes=[pltpu.VMEM((2, RW_HEAD, W), F32)],
        compiler_params=_params(("parallel", "arbitrary")),
        name="wkv_scan",
    )(g, h, rq, y0, g, h, rq, y0)


def _mix_ffn_kernel(x_ref, g1_ref, sh2_ref, sc2_ref, g2_ref,
                    yf_ref, yb_ref, gg_ref, bonus_ref, gng_ref, gnb_ref, bavg_ref,
                    att_ref, conv_ref, gate_ref,
                    woa_ref, woc_ref, wor_ref, wout_ref,
                    l1g_ref, l1b_ref, w13_ref, w2_ref, l2g_ref, l2b_ref,
                    o_ref, *, alpha):
    y = yf_ref[...] + yb_ref[...]
    bavg = bavg_ref[...]
    mu = _dot_exact_rhs(y, bavg) * (1.0 / RW_HEAD)
    yc = y - mu
    var = _dot_exact_rhs(yc * yc, bavg) * (1.0 / RW_HEAD)
    yn = yc * lax.rsqrt(var + RW_GN_EPS) * gng_ref[...] + gnb_ref[...]
    rwo = (yn + bonus_ref[...]) * gg_ref[...]

    gl = gate_ref[...]
    merged = (_sigmoid(gl[:, 0:D_MODEL]) * _bdot(att_ref[...], woa_ref[...])
              + _sigmoid(gl[:, D_MODEL:2 * D_MODEL]) * _bdot(conv_ref[...], woc_ref[...])
              + _sigmoid(gl[:, 2 * D_MODEL:]) * _bdot(rwo, wor_ref[...]))
    o = _bdot(merged, wout_ref[...])
    x = x_ref[...]
    x1 = _layer_norm(alpha * x + g1_ref[...] * o, l1g_ref[...], l1b_ref[...])

    hmod = x1 * (1.0 + sc2_ref[...]) + sh2_ref[...]
    ug = _bdot(hmod, w13_ref[...])
    u = ug[:, :D_FF]
    f = _bdot(u * _sigmoid(u) * ug[:, D_FF:], w2_ref[...])
    o_ref[...] = _layer_norm(alpha * x1 + g2_ref[...] * f, l2g_ref[...], l2b_ref[...])


def _mix_ffn(geom, layer, alpha, x, modv, yf, yb, g, bonus, att, conv, gate, prm, bavg):
    W = RW_WIDTH
    nt = geom.nt
    lsel = lambda b, t: (layer, 0, 0)
    vec = lambda w: pl.BlockSpec((None, 1, w), lsel)
    wspec = lambda r, c: _const_spec((None, r, c), lsel)
    return pl.pallas_call(
        functools.partial(_mix_ffn_kernel, alpha=alpha),
        grid=(geom.B, nt),
        in_specs=[
            _tok_spec(geom, D_MODEL),
            _mod_spec(geom, layer, 2), _mod_spec(geom, layer, 3), _mod_spec(geom, layer, 4),
            _mod_spec(geom, layer, 5),
            _tok_spec(geom, W), _tok_spec(geom, W),
            _tok_spec(geom, W), _tok_spec(geom, W), vec(W), vec(W),
            pl.BlockSpec((W, W), lambda b, t: (0, 0)),
            _tok_spec(geom, MLA_WIDTH), _tok_spec(geom, CONV_WIDTH), _tok_spec(geom, GATE_SEG),
            wspec(MLA_WIDTH, D_MODEL), wspec(CONV_WIDTH, D_MODEL), wspec(W, D_MODEL),
            wspec(D_MODEL, D_MODEL),
            vec(D_MODEL), vec(D_MODEL),
            wspec(D_MODEL, 2 * D_FF), wspec(D_FF, D_MODEL),
            vec(D_MODEL), vec(D_MODEL),
        ],
        out_specs=_tok_spec(geom, D_MODEL),
        out_shape=jax.ShapeDtypeStruct((geom.n, D_MODEL), F32),
        compiler_params=_params(("parallel", "parallel")),
        name="mix_ffn",
    )(x, modv, modv, modv, modv, yf, yb, g, bonus, prm["rw_gn_g"], prm["rw_gn_b"], bavg,
      att, conv, gate,
      prm["w_o_attn"], prm["w_o_conv"], prm["w_o_rwkv"], prm["w_out"],
      prm["ln1_g"], prm["ln1_b"], prm["ffn_w13"], prm["ffn_w2"], prm["ln2_g"], prm["ln2_b"])


_ROPE_SWAP = np.concatenate([np.arange(8, 16), np.arange(0, 8), np.arange(24, 32), np.arange(16, 24)])


def _rope_tables(geom):
    pos = jnp.arange(geom.seq)
    row = (pos // GRID_W).astype(F32)
    col = (pos % GRID_W).astype(F32)
    axis_dim = MLA_ROPE // 2
    inv = ROPE_BASE ** (-jnp.arange(0, axis_dim, 2, dtype=F32) / axis_dim)
    ar, ac = row[:, None] * inv, col[:, None] * inv
    cr, sr, cc, sc = jnp.cos(ar), jnp.sin(ar), jnp.cos(ac), jnp.sin(ac)
    cos32 = jnp.concatenate([cr, cr, cc, cc], -1)
    sin32 = jnp.concatenate([-sr, sr, -sc, sc], -1)
    ones = jnp.ones((geom.seq, MLA_NOPE), F32)
    zpad = jnp.zeros((geom.seq, HEAD_PAD - MLA_NOPE - MLA_ROPE), F32)
    cos_l = jnp.concatenate([ones, cos32, zpad], -1)
    sin_l = jnp.concatenate([jnp.zeros_like(ones), sin32, zpad], -1)
    cos_c = jnp.ones((geom.ctx_len, HEAD_PAD), F32)
    sin_c = jnp.zeros((geom.ctx_len, HEAD_PAD), F32)
    return jnp.concatenate([cos_c, cos_l], 0), jnp.concatenate([sin_c, sin_l], 0)


def _layout_weights(w_in, w_uq, w_ukv):
    L = w_in.shape[0]
    o_q, o_kv, o_kr = MLA_Q_LORA, MLA_Q_LORA + MLA_KV_LORA, MLA_Q_LORA + MLA_KV_LORA + MLA_ROPE
    krope = w_in[:, :, o_kv:o_kr]
    zl = jnp.zeros((L, D_MODEL, MLA_NOPE), F32)
    zr = jnp.zeros((L, D_MODEL, HEAD_PAD - MLA_NOPE - MLA_ROPE), F32)
    w_in_p = jnp.concatenate(
        [w_in[:, :, :o_kv], zl, krope, zr, zl, krope[:, :, _ROPE_SWAP], zr, w_in[:, :, o_kr:]], -1).astype(BF16)

    wq = w_uq.reshape(L, MLA_Q_LORA, MLA_HEADS, MLA_NOPE + MLA_ROPE)
    q_rope = wq[..., MLA_NOPE:]
    zq = jnp.zeros((L, MLA_Q_LORA, MLA_HEADS, HEAD_PAD - MLA_NOPE - MLA_ROPE), F32)
    zn = jnp.zeros((L, MLA_Q_LORA, MLA_HEADS, MLA_NOPE), F32)
    wq_p = jnp.concatenate([wq, zq], -1).reshape(L, MLA_Q_LORA, -1).astype(BF16)
    wqs_p = jnp.concatenate([zn, q_rope[..., _ROPE_SWAP], zq], -1).reshape(L, MLA_Q_LORA, -1).astype(BF16)

    wkv = w_ukv.reshape(L, MLA_KV_LORA, MLA_HEADS, MLA_NOPE + MLA_V)
    zk = jnp.zeros((L, MLA_KV_LORA, MLA_HEADS, HEAD_PAD - MLA_NOPE), F32)
    wk_p = jnp.concatenate([wkv[..., :MLA_NOPE], zk], -1).reshape(L, MLA_KV_LORA, -1).astype(BF16)
    wv_p = wkv[..., MLA_NOPE:].reshape(L, MLA_KV_LORA, -1).astype(BF16)
    return w_in_p, wq_p, wqs_p, wk_p, wv_p


def _head_block_ones():
    idx = np.arange(RW_WIDTH) // RW_HEAD
    return jnp.asarray((idx[:, None] == idx[None, :]).astype(np.float32), dtype=BF16)


def kernel(x, c, ctx, c_ctx, mod_w, mod_b, w_in, q_norm, w_uq, kv_norm, w_ukv, w_o_attn, conv_w, w_o_conv,
           rw_mu, rw_w0, rw_w_up, rw_a0, rw_a_up, rw_g_up, rw_k_k, rw_k_a, rw_r_k, rw_gn_g, rw_gn_b,
           w_o_rwkv, w_out, ln1_g, ln1_b, ffn_w13, ffn_w2, ln2_g, ln2_b):
    B, seq, _ = x.shape
    ctx_len = ctx.shape[1]
    L = mod_w.shape[0]
    geom = _Geom(B, ctx_len, seq)
    alpha = (2.0 * L) ** 0.25

    rows = -(-(B + 1) // HALO) * HALO
    c_all = jnp.concatenate([c, c_ctx[None, :], jnp.zeros((rows - B - 1, D_MODEL), F32)], 0)
    modv = _mod_vectors(c_all, mod_w, mod_b).reshape(L, rows, 1, -1)

    w_in_p, wq_p, wqs_p, wk_p, wv_p = _layout_weights(w_in, w_uq, w_ukv)
    cos_t, sin_t = _rope_tables(geom)
    bsum = _head_block_ones()
    vec3 = lambda a: a.reshape(L, 1, -1)
    prm = dict(
        rw_mu=vec3(rw_mu), rw_w0=rw_w0, rw_w_up=rw_w_up, rw_a0=rw_a0, rw_a_up=rw_a_up, rw_g_up=rw_g_up,
        rw_k_k=vec3(rw_k_k), rw_k_a=vec3(rw_k_a), rw_r_k=vec3(rw_r_k), conv_w=conv_w,
        rw_gn_g=vec3(rw_gn_g), rw_gn_b=vec3(rw_gn_b),
        w_o_attn=w_o_attn.astype(BF16), w_o_conv=w_o_conv.astype(BF16), w_o_rwkv=w_o_rwkv.astype(BF16),
        w_out=w_out.astype(BF16), ln1_g=vec3(ln1_g), ln1_b=vec3(ln1_b),
        ffn_w13=ffn_w13.astype(BF16), ffn_w2=ffn_w2.astype(BF16), ln2_g=vec3(ln2_g), ln2_b=vec3(ln2_b),
    )
    qn, kvn = vec3(q_norm), vec3(kv_norm)

    xs = jnp.concatenate([ctx, x], axis=1).reshape(geom.n, D_MODEL)
    for l in range(L):
        mla, conv, rw, gate = _in_proj(geom, l, xs, modv, w_in_p)
        q, k, v = _mla_proj(geom, l, mla, qn, kvn, wq_p, wqs_p, wk_p, wv_p, cos_t, sin_t)
        att = _attention(geom, q, k, v)
        r, vv, kk, g, bonus, cv, kd, bb, lw = _features(geom, l, rw, conv, prm, bsum)
        gm, hm, rq, y0 = _wkv_prep(geom, r, vv, kk, kd, bb, lw)
        yf, yb = _wkv_scan(geom, gm, hm, rq, y0)
        xs = _mix_ffn(geom, l, alpha, xs, modv, yf, yb, g, bonus, att, cv, gate, prm, bsum)
    return xs.reshape(B, geom.tt, D_MODEL)[:, ctx_len:, :]
```

```python
import functools

import numpy as np
import jax
import jax.numpy as jnp
from jax import lax
from jax.experimental import pallas as pl
from jax.experimental.pallas import tpu as pltpu

F32 = jnp.float32
BF16 = jnp.bfloat16

D_MODEL = 1024
GRID_W = 64
MLA_HEADS = 8
MLA_Q_LORA = 384
MLA_KV_LORA = 256
MLA_NOPE = 64
MLA_ROPE = 32
MLA_V = 64
MLA_WIDTH = MLA_HEADS * MLA_V
ROPE_BASE = 10000.0
CONV_WIDTH = 512
RW_HEADS = 8
RW_HEAD = 64
RW_WIDTH = RW_HEADS * RW_HEAD
RW_DECAY_LORA = 64
RW_ICLR_LORA = 64
RW_GATE_LORA = 128
RW_GN_EPS = 64e-5
RW_IN = 3 * RW_WIDTH + 2 * RW_DECAY_LORA + 2 * RW_ICLR_LORA + RW_GATE_LORA
N_BRANCH = 3
D_FF = 2816
LN_EPS = 1e-5
RMS_EPS = 1e-6

LANES = 128
HEAD_PAD = LANES
TM = 256
CHUNK = 64
PREP_INTERLEAVE = 4
HALO = 16
VMEM_LIMIT = 56 * 1024 * 1024

MLA_SEG = MLA_Q_LORA + MLA_KV_LORA + 2 * HEAD_PAD
CONV_SEG = 3 * CONV_WIDTH
GATE_SEG = N_BRANCH * D_MODEL
IN_SEGS = (MLA_SEG, CONV_SEG, RW_IN, GATE_SEG)
IN_OFFS = tuple(int(v) for v in np.cumsum((0,) + IN_SEGS))


def _bdot(a, b):
    return jnp.dot(a.astype(BF16), b.astype(BF16), preferred_element_type=F32)


def _split2(a):
    hi = a.astype(BF16)
    lo = (a - hi.astype(F32)).astype(BF16)
    return hi, lo


def _dot3(a, b, dims=None):
    ah, al = _split2(a)
    bh, bl = _split2(b)
    if dims is None:
        f = lambda u, v: jnp.dot(u, v, preferred_element_type=F32)
    else:
        f = lambda u, v: lax.dot_general(u, v, dims, preferred_element_type=F32)
    return f(ah, bh) + (f(ah, bl) + f(al, bh))


def _dot1(a, b, dims=None):
    a, b = a.astype(BF16), b.astype(BF16)
    if dims is None:
        return jnp.dot(a, b, preferred_element_type=F32)
    return lax.dot_general(a, b, dims, preferred_element_type=F32)


_dot_wkv = _dot1
_dot_inv = _dot1


def _dot_exact_rhs(a, b_bf16):
    a1, a2 = _split2(a)
    return jnp.dot(a1, b_bf16, preferred_element_type=F32) + jnp.dot(a2, b_bf16, preferred_element_type=F32)


def _sigmoid(x):
    return 0.5 * jnp.tanh(0.5 * x) + 0.5


def _layer_norm(x, g, b):
    mu = jnp.mean(x, -1, keepdims=True)
    xc = x - mu
    var = jnp.mean(xc * xc, -1, keepdims=True)
    return xc * lax.rsqrt(var + LN_EPS) * g + b


def _const_spec(shape, index_map):
    return pl.BlockSpec(shape, index_map, pipeline_mode=pl.Buffered(1))


def _params(sem):
    return pltpu.CompilerParams(dimension_semantics=sem, vmem_limit_bytes=VMEM_LIMIT)


def _mod_kernel(c_ref, w_ref, b_ref, o_ref):
    c = c_ref[...]
    s = c * _sigmoid(c)
    o_ref[...] = _dot3(s, w_ref[...]) + b_ref[...]


def _mod_vectors(c_all, mod_w, mod_b):
    L = mod_w.shape[0]
    R = c_all.shape[0]
    n = mod_w.shape[2] // D_MODEL
    return pl.pallas_call(
        _mod_kernel,
        grid=(L, n),
        in_specs=[
            pl.BlockSpec((R, D_MODEL), lambda l, j: (0, 0)),
            pl.BlockSpec((None, D_MODEL, D_MODEL), lambda l, j: (l, 0, j)),
            pl.BlockSpec((None, 1, D_MODEL), lambda l, j: (l, 0, j)),
        ],
        out_specs=pl.BlockSpec((None, R, D_MODEL), lambda l, j: (l, 0, j)),
        out_shape=jax.ShapeDtypeStruct((L, R, n * D_MODEL), F32),
        compiler_params=_params(("parallel", "parallel")),
        name="mod_vectors",
    )(c_all, mod_w, mod_b.reshape(L, 1, -1))


class _Geom:
    def __init__(self, B, ctx_len, seq):
        assert ctx_len % TM == 0 and seq % TM == 0 and seq % GRID_W == 0
        self.B, self.ctx_len, self.seq = B, ctx_len, seq
        self.tt = ctx_len + seq
        self.nct = ctx_len // TM
        self.nt = self.tt // TM
        self.n = B * self.tt

    def mod_row(self, b, t):
        return jnp.where(t < self.nct, self.B, b)


def _mod_spec(geom, layer, j):
    return pl.BlockSpec((None, None, 1, D_MODEL), lambda b, t: (layer, geom.mod_row(b, t), 0, j))


def _tok_spec(geom, width):
    return pl.BlockSpec((TM, width), lambda b, t: (b * geom.nt + t, 0))


def _in_proj_kernel(x_ref, sh_ref, sc_ref, w_ref, mla_ref, conv_ref, rw_ref, gate_ref):
    h = (x_ref[...] * (1.0 + sc_ref[...]) + sh_ref[...]).astype(BF16)
    for o_ref, lo, hi in zip((mla_ref, conv_ref, rw_ref, gate_ref), IN_OFFS[:-1], IN_OFFS[1:]):
        o_ref[...] = jnp.dot(h, w_ref[:, lo:hi], preferred_element_type=F32).astype(BF16)


def _in_proj(geom, layer, x, modv, w_in_p):
    return pl.pallas_call(
        _in_proj_kernel,
        grid=(geom.B, geom.nt),
        in_specs=[
            _tok_spec(geom, D_MODEL),
            _mod_spec(geom, layer, 0),
            _mod_spec(geom, layer, 1),
            _const_spec((None, D_MODEL, IN_OFFS[-1]), lambda b, t: (layer, 0, 0)),
        ],
        out_specs=[_tok_spec(geom, w) for w in IN_SEGS],
        out_shape=[jax.ShapeDtypeStruct((geom.n, w), BF16) for w in IN_SEGS],
        compiler_params=_params(("parallel", "parallel")),
        name="in_proj",
    )(x, modv, modv, w_in_p)


def _mla_kernel(m_ref, qn_ref, kvn_ref, wq_ref, wqs_ref, wk_ref, wv_ref, cos_ref, sin_ref,
                q_ref, k_ref, v_ref):
    m = m_ref[...].astype(F32)
    cq = m[:, :MLA_Q_LORA]
    ckv = m[:, MLA_Q_LORA:MLA_Q_LORA + MLA_KV_LORA]
    kslab = m[:, MLA_Q_LORA + MLA_KV_LORA:MLA_Q_LORA + MLA_KV_LORA + HEAD_PAD]
    kslab_sw = m[:, MLA_Q_LORA + MLA_KV_LORA + HEAD_PAD:]
    cqn = (cq * lax.rsqrt(jnp.mean(cq * cq, -1, keepdims=True) + RMS_EPS) * qn_ref[...]).astype(BF16)
    ckvn = (ckv * lax.rsqrt(jnp.mean(ckv * ckv, -1, keepdims=True) + RMS_EPS) * kvn_ref[...]).astype(BF16)
    cos_t = cos_ref[...]
    sin_t = sin_ref[...]
    qa = jnp.dot(cqn, wq_ref[...], preferred_element_type=F32)
    qb = jnp.dot(cqn, wqs_ref[...], preferred_element_type=F32)
    kn = jnp.dot(ckvn, wk_ref[...], preferred_element_type=F32)
    kr = kslab * cos_t + kslab_sw * sin_t
    scale = (MLA_NOPE + MLA_ROPE) ** -0.5
    for h in range(MLA_HEADS):
        sl = slice(h * HEAD_PAD, (h + 1) * HEAD_PAD)
        q_ref[:, sl] = ((qa[:, sl] * cos_t + qb[:, sl] * sin_t) * scale).astype(BF16)
        k_ref[:, sl] = (kn[:, sl] + kr).astype(BF16)
    lane = lax.broadcasted_iota(jnp.int32, (1, MLA_HEADS * HEAD_PAD), 1) % HEAD_PAD
    v_ref[...] = (jnp.dot(ckvn, wv_ref[...], preferred_element_type=F32)
                  + (lane == MLA_V).astype(F32)).astype(BF16)


def _mla_proj(geom, layer, mla, q_norm, kv_norm, wq, wqs, wk, wv, cos_t, sin_t):
    hw = MLA_HEADS * HEAD_PAD
    lsel = lambda b, t: (layer, 0, 0)
    return pl.pallas_call(
        _mla_kernel,
        grid=(geom.B, geom.nt),
        in_specs=[
            _tok_spec(geom, MLA_SEG),
            pl.BlockSpec((None, 1, MLA_Q_LORA), lsel),
            pl.BlockSpec((None, 1, MLA_KV_LORA), lsel),
            _const_spec((None, MLA_Q_LORA, hw), lsel),
            _const_spec((None, MLA_Q_LORA, hw), lsel),
            _const_spec((None, MLA_KV_LORA, hw), lsel),
            _const_spec((None, MLA_KV_LORA, hw), lsel),
            pl.BlockSpec((TM, HEAD_PAD), lambda b, t: (t, 0)),
            pl.BlockSpec((TM, HEAD_PAD), lambda b, t: (t, 0)),
        ],
        out_specs=[_tok_spec(geom, hw)] * 3,
        out_shape=[jax.ShapeDtypeStruct((geom.n, hw), BF16)] * 3,
        compiler_params=_params(("parallel", "parallel")),
        name="mla_proj",
    )(mla, q_norm, kv_norm, wq, wqs, wk, wv, cos_t, sin_t)


HEADS_PER_STEP = 4


def _attn_kernel(q_ref, k_ref, v_ref, o_ref, *, nct, ctx_len, tt):
    t = pl.program_id(2)

    def run(nk):
        outs = []
        for j in range(HEADS_PER_STEP):
            q = q_ref[:, j * HEAD_PAD:(j + 1) * HEAD_PAD]
            k = k_ref[0:nk, j * HEAD_PAD:(j + 1) * HEAD_PAD]
            v = v_ref[0:nk, j * HEAD_PAD:(j + 1) * HEAD_PAD]
            s = lax.dot_general(q, k, (((1,), (1,)), ((), ())), preferred_element_type=F32)
            p = jnp.exp(s - jnp.max(s, -1, keepdims=True))
            pv = jnp.dot(p.astype(BF16), v, preferred_element_type=F32)
            outs.append(pv[:, :MLA_V] / pv[:, MLA_V:MLA_V + 1])
        o_ref[...] = jnp.concatenate(outs, -1).astype(BF16)

    pl.when(t < nct)(lambda: run(ctx_len))
    pl.when(t >= nct)(lambda: run(tt))


def _attention(geom, q, k, v):
    qw = HEADS_PER_STEP * HEAD_PAD
    vw = HEADS_PER_STEP * MLA_V
    return pl.pallas_call(
        functools.partial(_attn_kernel, nct=geom.nct, ctx_len=geom.ctx_len, tt=geom.tt),
        grid=(geom.B, MLA_HEADS // HEADS_PER_STEP, geom.nt),
        in_specs=[
            pl.BlockSpec((TM, qw), lambda b, h, t: (b * geom.nt + t, h)),
            pl.BlockSpec((geom.tt, qw), lambda b, h, t: (b, h)),
            pl.BlockSpec((geom.tt, qw), lambda b, h, t: (b, h)),
        ],
        out_specs=pl.BlockSpec((TM, vw), lambda b, h, t: (b * geom.nt + t, h)),
        out_shape=jax.ShapeDtypeStruct((geom.n, MLA_WIDTH), BF16),
        compiler_params=_params(("parallel", "parallel", "parallel")),
        name="attn",
    )(q, k, v)


def _shift_prev(x, halo_row):
    rows = lax.broadcasted_iota(jnp.int32, x.shape, 0)
    return jnp.where(rows == 0, halo_row, pltpu.roll(x, 1, 0))


def _shift_next(x, halo_row):
    rows = lax.broadcasted_iota(jnp.int32, x.shape, 0)
    return jnp.where(rows == x.shape[0] - 1, halo_row, pltpu.roll(x, x.shape[0] - 1, 0))


def _feat_kernel(rw_ref, rwp_ref, rwn_ref, cv_ref, cvp_ref, cvn_ref,
                 mu_ref, w0_ref, wup_ref, a0_ref, aup_ref, gup_ref, kk_ref, ka_ref, rk_ref,
                 cw_ref, bsum_ref,
                 r_o, v_o, kk_o, g_o, bonus_o, conv_o, kd_o, b_o, lw_o, *, nct, nt):
    t = pl.program_id(1)
    first = jnp.logical_or(t == 0, t == nct).astype(F32)
    last = jnp.logical_or(t == nct - 1, t == nt - 1).astype(F32)
    keep_prev = 1.0 - first
    keep_next = 1.0 - last
    W = RW_WIDTH

    def gated(ref):
        blk = ref[...].astype(F32)
        return blk[..., 2 * CONV_WIDTH:] * blk[..., :CONV_WIDTH]
    cblk = cv_ref[...].astype(F32)
    up = cblk[:, 2 * CONV_WIDTH:] * cblk[:, :CONV_WIDTH]
    up_p = _shift_prev(up, gated(cvp_ref)[HALO - 1:HALO] * keep_prev)
    up_n = _shift_next(up, gated(cvn_ref)[0:1] * keep_next)
    cw = cw_ref[...]
    conv_o[...] = (cblk[:, CONV_WIDTH:2 * CONV_WIDTH]
                   * (up_p * cw[0:1] + up * cw[1:2] + up_n * cw[2:3])).astype(BF16)

    p = rw_ref[...].astype(F32)
    p_prev = _shift_prev(p, rwp_ref[HALO - 1:HALO, :].astype(F32) * keep_prev)
    p_next = _shift_next(p, rwn_ref[0:1, :].astype(F32) * keep_next)
    p = p + (0.5 * (p_prev + p_next) - p) * mu_ref[...]
    r = p[:, 0:W]
    k = p[:, W:2 * W]
    v = p[:, 2 * W:3 * W]
    wd = p[:, 3 * W:3 * W + 2 * RW_DECAY_LORA]
    ad = p[:, 3 * W + 2 * RW_DECAY_LORA:3 * W + 2 * RW_DECAY_LORA + 2 * RW_ICLR_LORA]
    gd = p[:, 3 * W + 2 * RW_DECAY_LORA + 2 * RW_ICLR_LORA:]

    bsum = bsum_ref[...]
    kkf = k * kk_ref[...]
    kk = kkf * lax.rsqrt(jnp.maximum(_dot_exact_rhs(kkf * kkf, bsum), 1e-24))
    r_o[...] = r.astype(BF16)
    v_o[...] = v.astype(BF16)
    kk_o[...] = kk.astype(BF16)
    g_o[...] = _dot3(_sigmoid(gd), gup_ref[...]).astype(BF16)
    bonus_o[...] = (_dot_exact_rhs(r * k * rk_ref[...], bsum) * v).astype(BF16)
    for z in range(2):
        wdz = jnp.tanh(wd[:, z * RW_DECAY_LORA:(z + 1) * RW_DECAY_LORA])
        adz = ad[:, z * RW_ICLR_LORA:(z + 1) * RW_ICLR_LORA]
        tz = w0_ref[z:z + 1, :] + _dot3(wdz, wup_ref[z])
        lw_o[z] = _sigmoid(tz) * (-float(np.exp(-0.5)))
        a = _sigmoid(a0_ref[z:z + 1, :] + _dot3(adz, aup_ref[z]))
        kd_o[z] = (k * (1.0 + (a - 1.0) * ka_ref[...])).astype(BF16)
        b_o[z] = (kk * a).astype(BF16)


def _features(geom, layer, rw, conv, prm, bsum):
    nt, n = geom.nt, geom.n
    blocks_per_tile = TM // HALO
    last_halo_block = n // HALO - 1

    def prev_spec(w):
        return pl.BlockSpec((HALO, w), lambda b, t: (jnp.maximum((b * nt + t) * blocks_per_tile - 1, 0), 0))

    def next_spec(w):
        return pl.BlockSpec((HALO, w), lambda b, t: (jnp.minimum((b * nt + t + 1) * blocks_per_tile, last_halo_block), 0))

    lsel2 = lambda b, t: (layer, 0, 0)
    lsel3 = lambda b, t: (layer, 0, 0, 0)
    W = RW_WIDTH
    tok = _tok_spec(geom, W)
    dir_spec = pl.BlockSpec((2, TM, W), lambda b, t: (0, b * nt + t, 0))
    tok_shape = jax.ShapeDtypeStruct((n, W), BF16)
    dir_shape = jax.ShapeDtypeStruct((2, n, W), BF16)
    return pl.pallas_call(
        functools.partial(_feat_kernel, nct=geom.nct, nt=nt),
        grid=(geom.B, nt),
        in_specs=[
            _tok_spec(geom, RW_IN), prev_spec(RW_IN), next_spec(RW_IN),
            _tok_spec(geom, CONV_SEG), prev_spec(CONV_SEG), next_spec(CONV_SEG),
            pl.BlockSpec((None, 1, RW_IN), lsel2),
            pl.BlockSpec((None, 2, W), lsel2),
            pl.BlockSpec((None, 2, RW_DECAY_LORA, W), lsel3),
            pl.BlockSpec((None, 2, W), lsel2),
            pl.BlockSpec((None, 2, RW_ICLR_LORA, W), lsel3),
            pl.BlockSpec((None, RW_GATE_LORA, W), lsel2),
            pl.BlockSpec((None, 1, W), lsel2),
            pl.BlockSpec((None, 1, W), lsel2),
            pl.BlockSpec((None, 1, W), lsel2),
            pl.BlockSpec((None, 3, CONV_WIDTH), lsel2),
            pl.BlockSpec((W, W), lambda b, t: (0, 0)),
        ],
        out_specs=[tok] * 6 + [dir_spec] * 3,
        out_shape=[tok_shape] * 6 + [dir_shape] * 2 + [jax.ShapeDtypeStruct((2, n, W), F32)],
        compiler_params=_params(("parallel", "parallel")),
        name="feat",
    )(rw, rw, rw, conv, conv, conv,
      prm["rw_mu"], prm["rw_w0"], prm["rw_w_up"], prm["rw_a0"], prm["rw_a_up"], prm["rw_g_up"],
      prm["rw_k_k"], prm["rw_k_a"], prm["rw_r_k"], prm["conv_w"], bsum)


def _tri_inverse_all(a_list, eye):
    xs = [eye - a for a in a_list]
    ps = list(a_list)
    for _ in range(int(np.log2(CHUNK)) - 1):
        ps = [_dot_inv(p, p) for p in ps]
        xs = [x + _dot_inv(x, p) for x, p in zip(xs, ps)]
    return xs


def _wkv_prep_kernel(r_ref, v_ref, kk_ref, kd_ref, b_ref, lw_ref, g_o, h_o, rq_o, y0_o):
    z = pl.program_id(0)
    fwd = z == 0
    C = CHUNK
    ri = lax.broadcasted_iota(jnp.int32, (C, C), 0)
    ci = lax.broadcasted_iota(jnp.int32, (C, C), 1)
    strict = (ci - ri) * jnp.where(fwd, 1, -1) < 0
    diag = ri == ci
    incl = jnp.logical_or(strict, diag)
    eye = diag.astype(F32)
    incl_bf = incl.astype(F32).astype(BF16)
    nt_dims = (((1,), (1,)), ((), ()))

    def prologue(c):
        rows = slice(c * C, (c + 1) * C)
        lw = lw_ref[rows, :]
        L = _dot_exact_rhs_lhs(incl_bf, lw)
        Ltot = jnp.where(fwd, L[C - 1:C, :], L[0:1, :])
        enL = jnp.exp(-L)
        etail = jnp.exp(Ltot - L)
        kd = kd_ref[rows, :]
        bb = b_ref[rows, :]
        return dict(
            rows=rows,
            kap=(kk_ref[rows, :] * jnp.exp(L - lw)).astype(BF16),
            rt=r_ref[rows, :] * jnp.exp(L),
            kt=(kd * enL).astype(BF16),
            bt=(bb * enL).astype(BF16),
            ktp_t=jnp.transpose(kd * etail).astype(BF16),
            btp_t=jnp.transpose(bb * etail).astype(BF16),
            ptot=jnp.exp(Ltot),
            vv=v_ref[rows, :].astype(BF16))

    heads = [slice(h * RW_HEAD, (h + 1) * RW_HEAD) for h in range(RW_HEADS)]
    for c0 in range(0, TM // C, PREP_INTERLEAVE):
        chunks = [prologue(c) for c in range(c0, c0 + PREP_INTERLEAVE)]
        items = [(ch, hs) for ch in chunks for hs in heads]
        aa = [_dot_wkv(jnp.concatenate([ch["kap"][:, hs], ch["rt"][:, hs].astype(BF16)], axis=0),
                       jnp.concatenate([ch["bt"][:, hs], ch["kt"][:, hs]], axis=0), nt_dims)
              for ch, hs in items]
        a_b = [jnp.where(strict, m[:C, :C], 0.0) for m in aa]
        a_kk = [jnp.concatenate([jnp.where(strict, m[:C, C:], 0.0), jnp.where(incl, m[C:, C:], 0.0)], axis=0)
                for m in aa]
        a_rb = [jnp.where(incl, m[C:, :C], 0.0) for m in aa]
        tinv = _tri_inverse_all(a_b, eye)
        avv = [_dot_wkv(m, ch["vv"][:, hs]) for m, (ch, hs) in zip(a_kk, items)]
        wu = [_dot_wkv(t, jnp.concatenate([ch["kap"][:, hs], av[:C].astype(BF16)], axis=1))
              for t, av, (ch, hs) in zip(tinv, avv, items)]
        rbwu = [_dot_wkv(m, x) for m, x in zip(a_rb, wu)]
        bwu = [_dot_wkv(ch["btp_t"][hs, :], x) for x, (ch, hs) in zip(wu, items)]
        kv = [_dot_wkv(ch["ktp_t"][hs, :], ch["vv"][:, hs]) for ch, hs in items]
        nh = RW_HEADS
        for i, ch in enumerate(chunks):
            sel = slice(i * nh, (i + 1) * nh)
            rows = ch["rows"]
            rq_o[rows, :] = (ch["rt"] - jnp.concatenate([m[:, :RW_HEAD] for m in rbwu[sel]], axis=1)).astype(BF16)
            y0_o[rows, :] = jnp.concatenate([av[C:] - m[:, RW_HEAD:] for av, m in zip(avv[sel], rbwu[sel])],
                                            axis=1).astype(BF16)
            g_o[rows, :] = jnp.concatenate([jnp.where(diag, ch["ptot"][:, hs], 0.0) - m[:, :RW_HEAD]
                                            for m, hs in zip(bwu[sel], heads)], axis=1).astype(BF16)
            h_o[rows, :] = jnp.concatenate([m1 - m2[:, RW_HEAD:] for m1, m2 in zip(kv[sel], bwu[sel])],
                                           axis=1).astype(BF16)


def _dot_exact_rhs_lhs(mask_bf16, x):
    x1, x2 = _split2(x)
    return jnp.dot(mask_bf16, x1, preferred_element_type=F32) + jnp.dot(mask_bf16, x2, preferred_element_type=F32)


def _wkv_prep(geom, r, v, kk, kd, b, lw):
    W = RW_WIDTH
    ntiles = geom.n // TM
    tok = pl.BlockSpec((TM, W), lambda z, i: (i, 0))
    dirs = pl.BlockSpec((None, TM, W), lambda z, i: (z, i, 0))
    shapes = [jax.ShapeDtypeStruct((2, geom.n, W), BF16)] * 4
    return pl.pallas_call(
        _wkv_prep_kernel,
        grid=(2, ntiles),
        in_specs=[tok, tok, tok, dirs, dirs, dirs],
        out_specs=[dirs] * 4,
        out_shape=shapes,
        compiler_params=_params(("parallel", "parallel")),
        name="wkv_prep",
    )(r, v, kk, kd, b, lw)


def _wkv_scan_kernel(gf_ref, hf_ref, rqf_ref, y0f_ref, gb_ref, hb_ref, rqb_ref, y0b_ref,
                     yf_o, yb_o, state):
    C = CHUNK
    nch = TM // C

    @pl.when(pl.program_id(1) == 0)
    def _():
        state[...] = jnp.zeros_like(state)

    heads = [slice(h * RW_HEAD, (h + 1) * RW_HEAD) for h in range(RW_HEADS)]
    dirs = ((gf_ref, hf_ref, rqf_ref, y0f_ref, yf_o), (gb_ref, hb_ref, rqb_ref, y0b_ref, yb_o))
    m = [state[0], state[1]]
    for c in range(nch):
        for z, (g_ref, h_ref, rq_ref, y0_ref, y_o) in enumerate(dirs):
            cc = c if z == 0 else nch - 1 - c
            rows = slice(cc * C, (cc + 1) * C)
            g = g_ref[rows, :]
            rq = rq_ref[rows, :]
            mb = m[z].astype(BF16)
            ys = [jnp.dot(rq[:, hs], mb[:, hs], preferred_element_type=F32) for hs in heads]
            ms = [jnp.dot(g[:, hs], mb[:, hs], preferred_element_type=F32) for hs in heads]
            y_o[rows, :] = jnp.concatenate(ys, -1) + y0_ref[rows, :]
            m[z] = jnp.concatenate(ms, -1) + h_ref[rows, :]
    state[0] = m[0]
    state[1] = m[1]


def _wkv_scan(geom, g, h, rq, y0):
    W = RW_WIDTH
    nt, nct = geom.nt, geom.nct

    def fwd_tile(b, t):
        return (0, b * nt + t, 0)

    def bwd_tile(b, t):
        return (1, b * nt + jnp.where(t < nct, nct - 1 - t, nt - 1 - (t - nct)), 0)

    fspec = pl.BlockSpec((None, TM, W), fwd_tile)
    bspec = pl.BlockSpec((None, TM, W), bwd_tile)
    return pl.pallas_call(
        _wkv_scan_kernel,
        grid=(geom.B, nt),
        in_specs=[fspec] * 4 + [bspec] * 4,
        out_specs=[pl.BlockSpec((TM, W), lambda b, t: fwd_tile(b, t)[1:]),
                   pl.BlockSpec((TM, W), lambda b, t: bwd_tile(b, t)[1:])],
        out_shape=[jax.ShapeDtypeStruct((geom.n, W), F32)] * 2,
        scratch_shapes=[pltpu.VMEM((2, RW_HEAD, W), F32)],
        compiler_params=_params(("parallel", "arbitrary")),
        name="wkv_scan",
    )(g, h, rq, y0, g, h, rq, y0)


def _mix_ffn_kernel(x_ref, g1_ref, sh2_ref, sc2_ref, g2_ref,
                    yf_ref, yb_ref, gg_ref, bonus_ref, gng_ref, gnb_ref, bavg_ref,
                    att_ref, conv_ref, gate_ref,
                    woa_ref, woc_ref, wor_ref, wout_ref,
                    l1g_ref, l1b_ref, w13_ref, w2_ref, l2g_ref, l2b_ref,
                    o_ref, *, alpha):
    y = yf_ref[...] + yb_ref[...]
    bavg = bavg_ref[...]
    mu = _dot_exact_rhs(y, bavg) * (1.0 / RW_HEAD)
    yc = y - mu
    var = _dot_exact_rhs(yc * yc, bavg) * (1.0 / RW_HEAD)
    yn = yc * lax.rsqrt(var + RW_GN_EPS) * gng_ref[...] + gnb_ref[...]
    rwo = (yn + bonus_ref[...]) * gg_ref[...]

    gl = gate_ref[...].astype(F32)
    merged = (_sigmoid(gl[:, 0:D_MODEL]) * _bdot(att_ref[...], woa_ref[...])
              + _sigmoid(gl[:, D_MODEL:2 * D_MODEL]) * _bdot(conv_ref[...], woc_ref[...])
              + _sigmoid(gl[:, 2 * D_MODEL:]) * _bdot(rwo, wor_ref[...]))
    o = _bdot(merged, wout_ref[...])
    x = x_ref[...]
    x1 = _layer_norm(alpha * x + g1_ref[...] * o, l1g_ref[...], l1b_ref[...])

    hmod = x1 * (1.0 + sc2_ref[...]) + sh2_ref[...]
    ug = _bdot(hmod, w13_ref[...])
    u = ug[:, :D_FF]
    f = _bdot(u * _sigmoid(u) * ug[:, D_FF:], w2_ref[...])
    o_ref[...] = _layer_norm(alpha * x1 + g2_ref[...] * f, l2g_ref[...], l2b_ref[...])


def _mix_ffn(geom, layer, alpha, x, modv, yf, yb, g, bonus, att, conv, gate, prm, bavg):
    W = RW_WIDTH
    nt = geom.nt
    lsel = lambda b, t: (layer, 0, 0)
    vec = lambda w: pl.BlockSpec((None, 1, w), lsel)
    wspec = lambda r, c: _const_spec((None, r, c), lsel)
    return pl.pallas_call(
        functools.partial(_mix_ffn_kernel, alpha=alpha),
        grid=(geom.B, nt),
        in_specs=[
            _tok_spec(geom, D_MODEL),
            _mod_spec(geom, layer, 2), _mod_spec(geom, layer, 3), _mod_spec(geom, layer, 4),
            _mod_spec(geom, layer, 5),
            _tok_spec(geom, W), _tok_spec(geom, W),
            _tok_spec(geom, W), _tok_spec(geom, W), vec(W), vec(W),
            pl.BlockSpec((W, W), lambda b, t: (0, 0)),
            _tok_spec(geom, MLA_WIDTH), _tok_spec(geom, CONV_WIDTH), _tok_spec(geom, GATE_SEG),
            wspec(MLA_WIDTH, D_MODEL), wspec(CONV_WIDTH, D_MODEL), wspec(W, D_MODEL),
            wspec(D_MODEL, D_MODEL),
            vec(D_MODEL), vec(D_MODEL),
            wspec(D_MODEL, 2 * D_FF), wspec(D_FF, D_MODEL),
            vec(D_MODEL), vec(D_MODEL),
        ],
        out_specs=_tok_spec(geom, D_MODEL),
        out_shape=jax.ShapeDtypeStruct((geom.n, D_MODEL), F32),
        compiler_params=_params(("parallel", "parallel")),
        name="mix_ffn",
    )(x, modv, modv, modv, modv, yf, yb, g, bonus, prm["rw_gn_g"], prm["rw_gn_b"], bavg,
      att, conv, gate,
      prm["w_o_attn"], prm["w_o_conv"], prm["w_o_rwkv"], prm["w_out"],
      prm["ln1_g"], prm["ln1_b"], prm["ffn_w13"], prm["ffn_w2"], prm["ln2_g"], prm["ln2_b"])


_ROPE_SWAP = np.concatenate([np.arange(8, 16), np.arange(0, 8), np.arange(24, 32), np.arange(16, 24)])


def _rope_tables(geom):
    pos = jnp.arange(geom.seq)
    row = (pos // GRID_W).astype(F32)
    col = (pos % GRID_W).astype(F32)
    axis_dim = MLA_ROPE // 2
    inv = ROPE_BASE ** (-jnp.arange(0, axis_dim, 2, dtype=F32) / axis_dim)
    ar, ac = row[:, None] * inv, col[:, None] * inv
    cr, sr, cc, sc = jnp.cos(ar), jnp.sin(ar), jnp.cos(ac), jnp.sin(ac)
    cos32 = jnp.concatenate([cr, cr, cc, cc], -1)
    sin32 = jnp.concatenate([-sr, sr, -sc, sc], -1)
    ones = jnp.ones((geom.seq, MLA_NOPE), F32)
    zpad = jnp.zeros((geom.seq, HEAD_PAD - MLA_NOPE - MLA_ROPE), F32)
    cos_l = jnp.concatenate([ones, cos32, zpad], -1)
    sin_l = jnp.concatenate([jnp.zeros_like(ones), sin32, zpad], -1)
    cos_c = jnp.ones((geom.ctx_len, HEAD_PAD), F32)
    sin_c = jnp.zeros((geom.ctx_len, HEAD_PAD), F32)
    return jnp.concatenate([cos_c, cos_l], 0), jnp.concatenate([sin_c, sin_l], 0)


def _layout_weights(w_in, w_uq, w_ukv):
    L = w_in.shape[0]
    o_q, o_kv, o_kr = MLA_Q_LORA, MLA_Q_LORA + MLA_KV_LORA, MLA_Q_LORA + MLA_KV_LORA + MLA_ROPE
    krope = w_in[:, :, o_kv:o_kr]
    zl = jnp.zeros((L, D_MODEL, MLA_NOPE), F32)
    zr = jnp.zeros((L, D_MODEL, HEAD_PAD - MLA_NOPE - MLA_ROPE), F32)
    w_in_p = jnp.concatenate(
        [w_in[:, :, :o_kv], zl, krope, zr, zl, krope[:, :, _ROPE_SWAP], zr, w_in[:, :, o_kr:]], -1).astype(BF16)

    wq = w_uq.reshape(L, MLA_Q_LORA, MLA_HEADS, MLA_NOPE + MLA_ROPE)
    q_rope = wq[..., MLA_NOPE:]
    zq = jnp.zeros((L, MLA_Q_LORA, MLA_HEADS, HEAD_PAD - MLA_NOPE - MLA_ROPE), F32)
    zn = jnp.zeros((L, MLA_Q_LORA, MLA_HEADS, MLA_NOPE), F32)
    wq_p = jnp.concatenate([wq, zq], -1).reshape(L, MLA_Q_LORA, -1).astype(BF16)
    wqs_p = jnp.concatenate([zn, q_rope[..., _ROPE_SWAP], zq], -1).reshape(L, MLA_Q_LORA, -1).astype(BF16)

    wkv = w_ukv.reshape(L, MLA_KV_LORA, MLA_HEADS, MLA_NOPE + MLA_V)
    zk = jnp.zeros((L, MLA_KV_LORA, MLA_HEADS, HEAD_PAD - MLA_NOPE), F32)
    wk_p = jnp.concatenate([wkv[..., :MLA_NOPE], zk], -1).reshape(L, MLA_KV_LORA, -1).astype(BF16)
    zv = jnp.zeros((L, MLA_KV_LORA, MLA_HEADS, HEAD_PAD - MLA_V), F32)
    wv_p = jnp.concatenate([wkv[..., MLA_NOPE:], zv], -1).reshape(L, MLA_KV_LORA, -1).astype(BF16)
    return w_in_p, wq_p, wqs_p, wk_p, wv_p


def _head_block_ones():
    idx = np.arange(RW_WIDTH) // RW_HEAD
    return jnp.asarray((idx[:, None] == idx[None, :]).astype(np.float32), dtype=BF16)


def kernel(x, c, ctx, c_ctx, mod_w, mod_b, w_in, q_norm, w_uq, kv_norm, w_ukv, w_o_attn, conv_w, w_o_conv,
           rw_mu, rw_w0, rw_w_up, rw_a0, rw_a_up, rw_g_up, rw_k_k, rw_k_a, rw_r_k, rw_gn_g, rw_gn_b,
           w_o_rwkv, w_out, ln1_g, ln1_b, ffn_w13, ffn_w2, ln2_g, ln2_b):
    B, seq, _ = x.shape
    ctx_len = ctx.shape[1]
    L = mod_w.shape[0]
    geom = _Geom(B, ctx_len, seq)
    alpha = (2.0 * L) ** 0.25

    rows = -(-(B + 1) // HALO) * HALO
    c_all = jnp.concatenate([c, c_ctx[None, :], jnp.zeros((rows - B - 1, D_MODEL), F32)], 0)
    modv = _mod_vectors(c_all, mod_w, mod_b).reshape(L, rows, 1, -1)

    w_in_p, wq_p, wqs_p, wk_p, wv_p = _layout_weights(w_in, w_uq, w_ukv)
    cos_t, sin_t = _rope_tables(geom)
    bsum = _head_block_ones()
    vec3 = lambda a: a.reshape(L, 1, -1)
    prm = dict(
        rw_mu=vec3(rw_mu), rw_w0=rw_w0, rw_w_up=rw_w_up, rw_a0=rw_a0, rw_a_up=rw_a_up, rw_g_up=rw_g_up,
        rw_k_k=vec3(rw_k_k), rw_k_a=vec3(rw_k_a), rw_r_k=vec3(rw_r_k), conv_w=conv_w,
        rw_gn_g=vec3(rw_gn_g), rw_gn_b=vec3(rw_gn_b),
        w_o_attn=w_o_attn.astype(BF16), w_o_conv=w_o_conv.astype(BF16), w_o_rwkv=w_o_rwkv.astype(BF16),
        w_out=w_out.astype(BF16), ln1_g=vec3(ln1_g), ln1_b=vec3(ln1_b),
        ffn_w13=ffn_w13.astype(BF16), ffn_w2=ffn_w2.astype(BF16), ln2_g=vec3(ln2_g), ln2_b=vec3(ln2_b),
    )
    qn, kvn = vec3(q_norm), vec3(kv_norm)

    xs = jnp.concatenate([ctx, x], axis=1).reshape(geom.n, D_MODEL)
    for l in range(L):
        mla, conv, rw, gate = _in_proj(geom, l, xs, modv, w_in_p)
        q, k, v = _mla_proj(geom, l, mla, qn, kvn, wq_p, wqs_p, wk_p, wv_p, cos_t, sin_t)
        att = _attention(geom, q, k, v)
        r, vv, kk, g, bonus, cv, kd, bb, lw = _features(geom, l, rw, conv, prm, bsum)
        gm, hm, rq, y0 = _wkv_prep(geom, r, vv, kk, kd, bb, lw)
        yf, yb = _wkv_scan(geom, gm, hm, rq, y0)
        xs = _mix_ffn(geom, l, alpha, xs, modv, yf, yb, g, bonus, att, cv, gate, prm, bsum)
    return xs.reshape(B, geom.tt, D_MODEL)[:, ctx_len:, :]
```

```python
import functools

import numpy as np
import jax
import jax.numpy as jnp
from jax import lax
from jax.experimental import pallas as pl
from jax.experimental.pallas import tpu as pltpu

F32 = jnp.float32
BF16 = jnp.bfloat16

D_MODEL = 1024
GRID_W = 64
MLA_HEADS = 8
MLA_Q_LORA = 384
MLA_KV_LORA = 256
MLA_NOPE = 64
MLA_ROPE = 32
MLA_V = 64
MLA_WIDTH = MLA_HEADS * MLA_V
ROPE_BASE = 10000.0
CONV_WIDTH = 512
RW_HEADS = 8
RW_HEAD = 64
RW_WIDTH = RW_HEADS * RW_HEAD
RW_DECAY_LORA = 64
RW_ICLR_LORA = 64
RW_GATE_LORA = 128
RW_GN_EPS = 64e-5
RW_IN = 3 * RW_WIDTH + 2 * RW_DECAY_LORA + 2 * RW_ICLR_LORA + RW_GATE_LORA
N_BRANCH = 3
D_FF = 2816
LN_EPS = 1e-5
RMS_EPS = 1e-6

LANES = 128
HEAD_PAD = LANES
TM = 256
CHUNK = 64
PAIR = 2 * RW_HEAD
assert PAIR == LANES
PREP_INTERLEAVE = 4
HALO = 16
VMEM_LIMIT = 56 * 1024 * 1024

MLA_SEG = MLA_Q_LORA + MLA_KV_LORA + 2 * HEAD_PAD
CONV_SEG = 3 * CONV_WIDTH
GATE_SEG = N_BRANCH * D_MODEL
IN_SEGS = (MLA_SEG, CONV_SEG, RW_IN, GATE_SEG)
IN_OFFS = tuple(int(v) for v in np.cumsum((0,) + IN_SEGS))


def _bdot(a, b):
    return jnp.dot(a.astype(BF16), b.astype(BF16), preferred_element_type=F32)


def _split2(a):
    hi = a.astype(BF16)
    lo = (a - hi.astype(F32)).astype(BF16)
    return hi, lo


def _dot3(a, b, dims=None):
    ah, al = _split2(a)
    bh, bl = _split2(b)
    if dims is None:
        f = lambda u, v: jnp.dot(u, v, preferred_element_type=F32)
    else:
        f = lambda u, v: lax.dot_general(u, v, dims, preferred_element_type=F32)
    return f(ah, bh) + (f(ah, bl) + f(al, bh))


def _dot1(a, b, dims=None):
    a, b = a.astype(BF16), b.astype(BF16)
    if dims is None:
        return jnp.dot(a, b, preferred_element_type=F32)
    return lax.dot_general(a, b, dims, preferred_element_type=F32)


_dot_wkv = _dot1


def _dot_exact_rhs(a, b_bf16):
    a1, a2 = _split2(a)
    return jnp.dot(a1, b_bf16, preferred_element_type=F32) + jnp.dot(a2, b_bf16, preferred_element_type=F32)


def _sigmoid(x):
    return 0.5 * jnp.tanh(0.5 * x) + 0.5


def _layer_norm(x, g, b):
    mu = jnp.mean(x, -1, keepdims=True)
    xc = x - mu
    var = jnp.mean(xc * xc, -1, keepdims=True)
    return xc * lax.rsqrt(var + LN_EPS) * g + b


def _const_spec(shape, index_map):
    return pl.BlockSpec(shape, index_map, pipeline_mode=pl.Buffered(1))


def _params(sem):
    return pltpu.CompilerParams(dimension_semantics=sem, vmem_limit_bytes=VMEM_LIMIT)


def _mod_kernel(c_ref, w_ref, b_ref, o_ref):
    c = c_ref[...]
    s = c * _sigmoid(c)
    o_ref[...] = _dot3(s, w_ref[...]) + b_ref[...]


def _mod_vectors(c_all, mod_w, mod_b):
    L = mod_w.shape[0]
    R = c_all.shape[0]
    n = mod_w.shape[2] // D_MODEL
    return pl.pallas_call(
        _mod_kernel,
        grid=(L, n),
        in_specs=[
            pl.BlockSpec((R, D_MODEL), lambda l, j: (0, 0)),
            pl.BlockSpec((None, D_MODEL, D_MODEL), lambda l, j: (l, 0, j)),
            pl.BlockSpec((None, 1, D_MODEL), lambda l, j: (l, 0, j)),
        ],
        out_specs=pl.BlockSpec((None, R, D_MODEL), lambda l, j: (l, 0, j)),
        out_shape=jax.ShapeDtypeStruct((L, R, n * D_MODEL), F32),
        compiler_params=_params(("parallel", "parallel")),
        name="mod_vectors",
    )(c_all, mod_w, mod_b.reshape(L, 1, -1))


class _Geom:
    def __init__(self, B, ctx_len, seq):
        assert ctx_len % TM == 0 and seq % TM == 0 and seq % GRID_W == 0
        self.B, self.ctx_len, self.seq = B, ctx_len, seq
        self.tt = ctx_len + seq
        self.nct = ctx_len // TM
        self.nt = self.tt // TM
        self.n = B * self.tt

    def mod_row(self, b, t):
        return jnp.where(t < self.nct, self.B, b)


def _mod_spec(geom, layer, j):
    return pl.BlockSpec((None, None, 1, D_MODEL), lambda b, t: (layer, geom.mod_row(b, t), 0, j))


def _tok_spec(geom, width):
    return pl.BlockSpec((TM, width), lambda b, t: (b * geom.nt + t, 0))


def _in_proj_kernel(x_ref, sh_ref, sc_ref, w_ref, mla_ref, conv_ref, rw_ref, gate_ref):
    h = (x_ref[...] * (1.0 + sc_ref[...]) + sh_ref[...]).astype(BF16)
    for o_ref, lo, hi in zip((mla_ref, conv_ref, rw_ref, gate_ref), IN_OFFS[:-1], IN_OFFS[1:]):
        o_ref[...] = jnp.dot(h, w_ref[:, lo:hi], preferred_element_type=F32).astype(BF16)


def _in_proj(geom, layer, x, modv, w_in_p):
    return pl.pallas_call(
        _in_proj_kernel,
        grid=(geom.B, geom.nt),
        in_specs=[
            _tok_spec(geom, D_MODEL),
            _mod_spec(geom, layer, 0),
            _mod_spec(geom, layer, 1),
            _const_spec((None, D_MODEL, IN_OFFS[-1]), lambda b, t: (layer, 0, 0)),
        ],
        out_specs=[_tok_spec(geom, w) for w in IN_SEGS],
        out_shape=[jax.ShapeDtypeStruct((geom.n, w), BF16) for w in IN_SEGS],
        compiler_params=_params(("parallel", "parallel")),
        name="in_proj",
    )(x, modv, modv, w_in_p)


def _mla_kernel(m_ref, qn_ref, kvn_ref, wq_ref, wqs_ref, wk_ref, wv_ref, cos_ref, sin_ref,
                q_ref, k_ref, v_ref):
    m = m_ref[...].astype(F32)
    cq = m[:, :MLA_Q_LORA]
    ckv = m[:, MLA_Q_LORA:MLA_Q_LORA + MLA_KV_LORA]
    kslab = m[:, MLA_Q_LORA + MLA_KV_LORA:MLA_Q_LORA + MLA_KV_LORA + HEAD_PAD]
    kslab_sw = m[:, MLA_Q_LORA + MLA_KV_LORA + HEAD_PAD:]
    cqn = (cq * lax.rsqrt(jnp.mean(cq * cq, -1, keepdims=True) + RMS_EPS) * qn_ref[...]).astype(BF16)
    ckvn = (ckv * lax.rsqrt(jnp.mean(ckv * ckv, -1, keepdims=True) + RMS_EPS) * kvn_ref[...]).astype(BF16)
    cos_t = cos_ref[...]
    sin_t = sin_ref[...]
    qa = jnp.dot(cqn, wq_ref[...], preferred_element_type=F32)
    qb = jnp.dot(cqn, wqs_ref[...], preferred_element_type=F32)
    kn = jnp.dot(ckvn, wk_ref[...], preferred_element_type=F32)
    kr = kslab * cos_t + kslab_sw * sin_t
    scale = (MLA_NOPE + MLA_ROPE) ** -0.5
    for h in range(MLA_HEADS):
        sl = slice(h * HEAD_PAD, (h + 1) * HEAD_PAD)
        q_ref[:, sl] = ((qa[:, sl] * cos_t + qb[:, sl] * sin_t) * scale).astype(BF16)
        k_ref[:, sl] = (kn[:, sl] + kr).astype(BF16)
    lane = lax.broadcasted_iota(jnp.int32, (1, MLA_HEADS * HEAD_PAD), 1) % HEAD_PAD
    v_ref[...] = (jnp.dot(ckvn, wv_ref[...], preferred_element_type=F32)
                  + (lane == MLA_V).astype(F32)).astype(BF16)


def _mla_proj(geom, layer, mla, q_norm, kv_norm, wq, wqs, wk, wv, cos_t, sin_t):
    hw = MLA_HEADS * HEAD_PAD
    lsel = lambda b, t: (layer, 0, 0)
    return pl.pallas_call(
        _mla_kernel,
        grid=(geom.B, geom.nt),
        in_specs=[
            _tok_spec(geom, MLA_SEG),
            pl.BlockSpec((None, 1, MLA_Q_LORA), lsel),
            pl.BlockSpec((None, 1, MLA_KV_LORA), lsel),
            _const_spec((None, MLA_Q_LORA, hw), lsel),
            _const_spec((None, MLA_Q_LORA, hw), lsel),
            _const_spec((None, MLA_KV_LORA, hw), lsel),
            _const_spec((None, MLA_KV_LORA, hw), lsel),
            pl.BlockSpec((TM, HEAD_PAD), lambda b, t: (t, 0)),
            pl.BlockSpec((TM, HEAD_PAD), lambda b, t: (t, 0)),
        ],
        out_specs=[_tok_spec(geom, hw)] * 3,
        out_shape=[jax.ShapeDtypeStruct((geom.n, hw), BF16)] * 3,
        compiler_params=_params(("parallel", "parallel")),
        name="mla_proj",
    )(mla, q_norm, kv_norm, wq, wqs, wk, wv, cos_t, sin_t)


HEADS_PER_STEP = 4


def _attn_kernel(q_ref, k_ref, v_ref, o_ref, *, nct, ctx_len, tt):
    t = pl.program_id(2)

    def run(nk):
        outs = []
        for j in range(HEADS_PER_STEP):
            q = q_ref[:, j * HEAD_PAD:(j + 1) * HEAD_PAD]
            k = k_ref[0:nk, j * HEAD_PAD:(j + 1) * HEAD_PAD]
            v = v_ref[0:nk, j * HEAD_PAD:(j + 1) * HEAD_PAD]
            s = lax.dot_general(q, k, (((1,), (1,)), ((), ())), preferred_element_type=F32)
            p = jnp.exp(s - jnp.max(s, -1, keepdims=True))
            pv = jnp.dot(p.astype(BF16), v, preferred_element_type=F32)
            outs.append(pv[:, :MLA_V] / pv[:, MLA_V:MLA_V + 1])
        o_ref[...] = jnp.concatenate(outs, -1).astype(BF16)

    pl.when(t < nct)(lambda: run(ctx_len))
    pl.when(t >= nct)(lambda: run(tt))


def _attention(geom, q, k, v):
    qw = HEADS_PER_STEP * HEAD_PAD
    vw = HEADS_PER_STEP * MLA_V
    return pl.pallas_call(
        functools.partial(_attn_kernel, nct=geom.nct, ctx_len=geom.ctx_len, tt=geom.tt),
        grid=(geom.B, MLA_HEADS // HEADS_PER_STEP, geom.nt),
        in_specs=[
            pl.BlockSpec((TM, qw), lambda b, h, t: (b * geom.nt + t, h)),
            pl.BlockSpec((geom.tt, qw), lambda b, h, t: (b, h)),
            pl.BlockSpec((geom.tt, qw), lambda b, h, t: (b, h)),
        ],
        out_specs=pl.BlockSpec((TM, vw), lambda b, h, t: (b * geom.nt + t, h)),
        out_shape=jax.ShapeDtypeStruct((geom.n, MLA_WIDTH), BF16),
        compiler_params=_params(("parallel", "parallel", "parallel")),
        name="attn",
    )(q, k, v)


def _shift_prev(x, halo_row):
    rows = lax.broadcasted_iota(jnp.int32, x.shape, 0)
    return jnp.where(rows == 0, halo_row, pltpu.roll(x, 1, 0))


def _shift_next(x, halo_row):
    rows = lax.broadcasted_iota(jnp.int32, x.shape, 0)
    return jnp.where(rows == x.shape[0] - 1, halo_row, pltpu.roll(x, x.shape[0] - 1, 0))


def _feat_kernel(rw_ref, rwp_ref, rwn_ref, cv_ref, cvp_ref, cvn_ref,
                 mu_ref, w0_ref, wup_ref, a0_ref, aup_ref, gup_ref, kk_ref, ka_ref, rk_ref,
                 cw_ref, bsum_ref,
                 g_o, bonus_o, conv_o, gm_o, hm_o, rq_o, y0_o, *, nct, nt):
    t = pl.program_id(1)
    first = jnp.logical_or(t == 0, t == nct).astype(F32)
    last = jnp.logical_or(t == nct - 1, t == nt - 1).astype(F32)
    keep_prev = 1.0 - first
    keep_next = 1.0 - last
    W = RW_WIDTH

    def gated(ref):
        blk = ref[...].astype(F32)
        return blk[..., 2 * CONV_WIDTH:] * blk[..., :CONV_WIDTH]
    cblk = cv_ref[...].astype(F32)
    up = cblk[:, 2 * CONV_WIDTH:] * cblk[:, :CONV_WIDTH]
    up_p = _shift_prev(up, gated(cvp_ref)[HALO - 1:HALO] * keep_prev)
    up_n = _shift_next(up, gated(cvn_ref)[0:1] * keep_next)
    cw = cw_ref[...]
    conv_o[...] = (cblk[:, CONV_WIDTH:2 * CONV_WIDTH]
                   * (up_p * cw[0:1] + up * cw[1:2] + up_n * cw[2:3])).astype(BF16)

    p = rw_ref[...].astype(F32)
    p_prev = _shift_prev(p, rwp_ref[HALO - 1:HALO, :].astype(F32) * keep_prev)
    p_next = _shift_next(p, rwn_ref[0:1, :].astype(F32) * keep_next)
    p = p + (0.5 * (p_prev + p_next) - p) * mu_ref[...]
    r = p[:, 0:W]
    k = p[:, W:2 * W]
    v = p[:, 2 * W:3 * W]
    wd = p[:, 3 * W:3 * W + 2 * RW_DECAY_LORA]
    ad = p[:, 3 * W + 2 * RW_DECAY_LORA:3 * W + 2 * RW_DECAY_LORA + 2 * RW_ICLR_LORA]
    gd = p[:, 3 * W + 2 * RW_DECAY_LORA + 2 * RW_ICLR_LORA:]

    bsum = bsum_ref[...]
    kkf = k * kk_ref[...]
    kk = kkf * lax.rsqrt(jnp.maximum(_dot_exact_rhs(kkf * kkf, bsum), 1e-24))
    g_o[...] = _dot3(_sigmoid(gd), gup_ref[...]).astype(BF16)
    bonus_o[...] = (_dot_exact_rhs(r * k * rk_ref[...], bsum) * v).astype(BF16)
    for z in range(2):
        wdz = jnp.tanh(wd[:, z * RW_DECAY_LORA:(z + 1) * RW_DECAY_LORA])
        adz = ad[:, z * RW_ICLR_LORA:(z + 1) * RW_ICLR_LORA]
        tz = w0_ref[z:z + 1, :] + _dot3(wdz, wup_ref[z])
        lw = _sigmoid(tz) * (-float(np.exp(-0.5)))
        a = _sigmoid(a0_ref[z:z + 1, :] + _dot3(adz, aup_ref[z]))
        kd = k * (1.0 + (a - 1.0) * ka_ref[...])
        _wkv_chunk_reduce_pairs(z, r, v, kk, kd, kk * a, lw, gm_o, hm_o, rq_o, y0_o)


def _features(geom, layer, rw, conv, prm, bsum):
    nt, n = geom.nt, geom.n
    blocks_per_tile = TM // HALO
    last_halo_block = n // HALO - 1

    def prev_spec(w):
        return pl.BlockSpec((HALO, w), lambda b, t: (jnp.maximum((b * nt + t) * blocks_per_tile - 1, 0), 0))

    def next_spec(w):
        return pl.BlockSpec((HALO, w), lambda b, t: (jnp.minimum((b * nt + t + 1) * blocks_per_tile, last_halo_block), 0))

    lsel2 = lambda b, t: (layer, 0, 0)
    lsel3 = lambda b, t: (layer, 0, 0, 0)
    W = RW_WIDTH
    tok = _tok_spec(geom, W)
    dir_spec = pl.BlockSpec((2, TM, W), lambda b, t: (0, b * nt + t, 0))
    tok_shape = jax.ShapeDtypeStruct((n, W), BF16)
    dir_shape = jax.ShapeDtypeStruct((2, n, W), BF16)
    smap_rows = TM // CHUNK * PAIR
    smap_spec = pl.BlockSpec((2, smap_rows, W), lambda b, t: (0, b * nt + t, 0))
    smap_shape = jax.ShapeDtypeStruct((2, n // TM * smap_rows, W), BF16)
    return pl.pallas_call(
        functools.partial(_feat_kernel, nct=geom.nct, nt=nt),
        grid=(geom.B, nt),
        in_specs=[
            _tok_spec(geom, RW_IN), prev_spec(RW_IN), next_spec(RW_IN),
            _tok_spec(geom, CONV_SEG), prev_spec(CONV_SEG), next_spec(CONV_SEG),
            pl.BlockSpec((None, 1, RW_IN), lsel2),
            pl.BlockSpec((None, 2, W), lsel2),
            pl.BlockSpec((None, 2, RW_DECAY_LORA, W), lsel3),
            pl.BlockSpec((None, 2, W), lsel2),
            pl.BlockSpec((None, 2, RW_ICLR_LORA, W), lsel3),
            pl.BlockSpec((None, RW_GATE_LORA, W), lsel2),
            pl.BlockSpec((None, 1, W), lsel2),
            pl.BlockSpec((None, 1, W), lsel2),
            pl.BlockSpec((None, 1, W), lsel2),
            pl.BlockSpec((None, 3, CONV_WIDTH), lsel2),
            pl.BlockSpec((W, W), lambda b, t: (0, 0)),
        ],
        out_specs=[tok] * 3 + [smap_spec] * 2 + [dir_spec] * 2,
        out_shape=[tok_shape] * 3 + [smap_shape] * 2 + [dir_shape] * 2,
        compiler_params=_params(("parallel", "parallel")),
        name="feat_wkv_prep",
    )(rw, rw, rw, conv, conv, conv,
      prm["rw_mu"], prm["rw_w0"], prm["rw_w_up"], prm["rw_a0"], prm["rw_a_up"], prm["rw_g_up"],
      prm["rw_k_k"], prm["rw_k_a"], prm["rw_r_k"], prm["conv_w"], bsum)


def _block_diag(x, lo, hi):
    zero = jnp.zeros_like(x)
    return jnp.concatenate([jnp.where(lo, x, zero), jnp.where(hi, x, zero)], axis=0)


def _wkv_chunk_reduce_pairs(z, r_all, v_all, kk_all, kd_all, b_all, lw_all, g_o, h_o, rq_o, y0_o):
    fwd = z == 0
    C = CHUNK
    assert C == RW_HEAD
    ri = lax.broadcasted_iota(jnp.int32, (C, C), 0)
    ci = lax.broadcasted_iota(jnp.int32, (C, C), 1)
    incl_bf = ((ci <= ri) if fwd else (ci >= ri)).astype(F32).astype(BF16)
    rp = lax.broadcasted_iota(jnp.int32, (C, PAIR), 0)
    cp = lax.broadcasted_iota(jnp.int32, (C, PAIR), 1) % C
    strict = cp < rp if fwd else cp > rp
    incl = cp <= rp if fwd else cp >= rp
    eye = (cp == rp).astype(F32)
    lane1 = lax.broadcasted_iota(jnp.int32, (1, PAIR), 1)
    lane2 = lax.broadcasted_iota(jnp.int32, (1, 2 * PAIR), 1) % PAIR
    lo1, hi1 = lane1 < RW_HEAD, lane1 >= RW_HEAD
    lo2, hi2 = lane2 < RW_HEAD, lane2 >= RW_HEAD
    lane_even = lax.broadcasted_iota(jnp.int32, (1, RW_WIDTH), 1) % PAIR < RW_HEAD
    r2 = lax.broadcasted_iota(jnp.int32, (PAIR, PAIR), 0)
    c2 = lax.broadcasted_iota(jnp.int32, (PAIR, PAIR), 1)
    same_head = (r2 < RW_HEAD) == (c2 < RW_HEAD)
    eye2 = r2 == c2
    nt_dims = (((1,), (1,)), ((), ()))
    npair = RW_WIDTH // PAIR
    pairs = [slice(p * PAIR, (p + 1) * PAIR) for p in range(npair)]

    def prologue(c):
        rows = slice(c * C, (c + 1) * C)
        lw = lw_all[rows, :]
        L = _dot_exact_rhs_lhs(incl_bf, lw)
        Ltot = L[C - 1:C, :] if fwd else L[0:1, :]
        enL = jnp.exp(-L)
        etail = jnp.exp(Ltot - L)
        kd = kd_all[rows, :]
        bb = b_all[rows, :]
        rt = r_all[rows, :] * jnp.exp(L)
        bt = (bb * enL).astype(BF16)
        kt = (kd * enL).astype(BF16)
        zero = jnp.zeros_like(bt)
        return dict(
            c=c, rows=rows, rt=rt,
            lhs=jnp.concatenate([kk_all[rows, :] * jnp.exp(L - lw), rt], axis=0).astype(BF16),
            rhs=jnp.concatenate([jnp.where(lane_even, bt, zero), jnp.where(lane_even, zero, bt),
                                 jnp.where(lane_even, kt, zero), jnp.where(lane_even, zero, kt)], axis=0),
            ktp_t=jnp.transpose(kd * etail).astype(BF16),
            btp_t=jnp.transpose(bb * etail).astype(BF16),
            ptot=jnp.exp(Ltot),
            vv=v_all[rows, :].astype(BF16))

    for c0 in range(0, TM // C, PREP_INTERLEAVE):
        chunks = [prologue(c) for c in range(c0, c0 + PREP_INTERLEAVE)]
        items = [(ch, ps) for ch in chunks for ps in pairs]
        aa = [_dot_wkv(ch["lhs"][:, ps], ch["rhs"][:, ps], nt_dims) for ch, ps in items]
        a_b = [jnp.where(strict, m[:C, :PAIR], 0.0) for m in aa]
        a_kk = [jnp.concatenate([jnp.where(strict, m[:C, PAIR:], 0.0), jnp.where(incl, m[C:, PAIR:], 0.0)], axis=0)
                .astype(BF16) for m in aa]
        a_rb = [jnp.where(incl, m[C:, :PAIR], 0.0).astype(BF16) for m in aa]
        xs = [eye - a for a in a_b]
        pw = a_b
        for _ in range(int(np.log2(C)) - 1):
            pd = [_block_diag(p.astype(BF16), lo1, hi1) for p in pw]
            pw = [_dot_wkv(p, d) for p, d in zip(pw, pd)]
            xs = [x + _dot_wkv(x, _block_diag(p.astype(BF16), lo1, hi1)) for x, p in zip(xs, pw)]
        avv = [_dot_wkv(m, _block_diag(ch["vv"][:, ps], lo1, hi1)) for m, (ch, ps) in zip(a_kk, items)]
        wu = [_dot_wkv(t, _block_diag(jnp.concatenate([ch["lhs"][:C, ps], av[:C].astype(BF16)], axis=1), lo2, hi2))
              .astype(BF16) for t, av, (ch, ps) in zip(xs, avv, items)]
        rbwu = [_dot_wkv(m, _block_diag(x, lo2, hi2)) for m, x in zip(a_rb, wu)]
        bwu = [_dot_wkv(ch["btp_t"][ps, :], x) for x, (ch, ps) in zip(wu, items)]
        kv = [_dot_wkv(ch["ktp_t"][ps, :], ch["vv"][:, ps]) for ch, ps in items]
        for av, rb, bw, kvp, (ch, ps) in zip(avv, rbwu, bwu, kv, items):
            rows = ch["rows"]
            srows = slice(ch["c"] * PAIR, (ch["c"] + 1) * PAIR)
            rq_o[z, rows, ps] = (ch["rt"][:, ps] - rb[:, :PAIR]).astype(BF16)
            y0_o[z, rows, ps] = (av[C:] - rb[:, PAIR:]).astype(BF16)
            dg = jnp.where(eye2, ch["ptot"][:, ps], 0.0)
            g_o[z, srows, ps] = jnp.where(same_head, dg - bw[:, :PAIR], 0.0).astype(BF16)
            h_o[z, srows, ps] = jnp.where(same_head, kvp - bw[:, PAIR:], 0.0).astype(BF16)


def _dot_exact_rhs_lhs(mask_bf16, x):
    x1, x2 = _split2(x)
    return jnp.dot(mask_bf16, x1, preferred_element_type=F32) + jnp.dot(mask_bf16, x2, preferred_element_type=F32)


def _wkv_scan_kernel(gf_ref, hf_ref, rqf_ref, y0f_ref, gb_ref, hb_ref, rqb_ref, y0b_ref,
                     yf_o, yb_o, state):
    C = CHUNK
    nch = TM // C

    @pl.when(pl.program_id(1) == 0)
    def _():
        state[...] = jnp.zeros_like(state)

    pairs = [slice(p * PAIR, (p + 1) * PAIR) for p in range(RW_WIDTH // PAIR)]
    dirs = ((gf_ref, hf_ref, rqf_ref, y0f_ref, yf_o), (gb_ref, hb_ref, rqb_ref, y0b_ref, yb_o))
    m = [state[0], state[1]]
    for c in range(nch):
        for z, (g_ref, h_ref, rq_ref, y0_ref, y_o) in enumerate(dirs):
            cc = c if z == 0 else nch - 1 - c
            rows = slice(cc * C, (cc + 1) * C)
            srows = slice(cc * PAIR, (cc + 1) * PAIR)
            g = g_ref[srows, :]
            rq = rq_ref[rows, :]
            mb = m[z].astype(BF16)
            ys = [jnp.dot(rq[:, ps], mb[:, ps], preferred_element_type=F32) for ps in pairs]
            ms = [jnp.dot(g[:, ps], mb[:, ps], preferred_element_type=F32) for ps in pairs]
            y_o[rows, :] = jnp.concatenate(ys, -1) + y0_ref[rows, :]
            m[z] = jnp.concatenate(ms, -1) + h_ref[srows, :]
    state[0] = m[0]
    state[1] = m[1]


def _wkv_scan(geom, g, h, rq, y0):
    W = RW_WIDTH
    nt, nct = geom.nt, geom.nct

    def fwd_tile(b, t):
        return (0, b * nt + t, 0)

    def bwd_tile(b, t):
        return (1, b * nt + jnp.where(t < nct, nct - 1 - t, nt - 1 - (t - nct)), 0)

    fspec = pl.BlockSpec((None, TM, W), fwd_tile)
    bspec = pl.BlockSpec((None, TM, W), bwd_tile)
    srows = TM // CHUNK * PAIR
    fsspec = pl.BlockSpec((None, srows, W), fwd_tile)
    bsspec = pl.BlockSpec((None, srows, W), bwd_tile)
    return pl.pallas_call(
        _wkv_scan_kernel,
        grid=(geom.B, nt),
        in_specs=[fsspec, fsspec, fspec, fspec, bsspec, bsspec, bspec, bspec],
        out_specs=[pl.BlockSpec((TM, W), lambda b, t: fwd_tile(b, t)[1:]),
                   pl.BlockSpec((TM, W), lambda b, t: bwd_tile(b, t)[1:])],
        out_shape=[jax.ShapeDtypeStruct((geom.n, W), F32)] * 2,
        scratch_shapes=[pltpu.VMEM((2, PAIR, W), F32)],
        compiler_params=_params(("parallel", "arbitrary")),
        name="wkv_scan",
    )(g, h, rq, y0, g, h, rq, y0)


def _mix_ffn_kernel(x_ref, g1_ref, sh2_ref, sc2_ref, g2_ref,
                    yf_ref, yb_ref, gg_ref, bonus_ref, gng_ref, gnb_ref, bavg_ref,
                    att_ref, conv_ref, gate_ref,
                    woa_ref, woc_ref, wor_ref, wout_ref,
                    l1g_ref, l1b_ref, w13_ref, w2_ref, l2g_ref, l2b_ref,
                    o_ref, *, alpha):
    y = yf_ref[...] + yb_ref[...]
    bavg = bavg_ref[...]
    mu = _dot_exact_rhs(y, bavg) * (1.0 / RW_HEAD)
    yc = y - mu
    var = _dot_exact_rhs(yc * yc, bavg) * (1.0 / RW_HEAD)
    yn = yc * lax.rsqrt(var + RW_GN_EPS) * gng_ref[...] + gnb_ref[...]
    rwo = (yn + bonus_ref[...]) * gg_ref[...]

    gl = gate_ref[...].astype(F32)
    merged = (_sigmoid(gl[:, 0:D_MODEL]) * _bdot(att_ref[...], woa_ref[...])
              + _sigmoid(gl[:, D_MODEL:2 * D_MODEL]) * _bdot(conv_ref[...], woc_ref[...])
              + _sigmoid(gl[:, 2 * D_MODEL:]) * _bdot(rwo, wor_ref[...]))
    o = _bdot(merged, wout_ref[...])
    x = x_ref[...]
    x1 = _layer_norm(alpha * x + g1_ref[...] * o, l1g_ref[...], l1b_ref[...])

    hmod = x1 * (1.0 + sc2_ref[...]) + sh2_ref[...]
    ug = _bdot(hmod, w13_ref[...])
    u = ug[:, :D_FF]
    f = _bdot(u * _sigmoid(u) * ug[:, D_FF:], w2_ref[...])
    o_ref[...] = _layer_norm(alpha * x1 + g2_ref[...] * f, l2g_ref[...], l2b_ref[...])


def _mix_ffn(geom, layer, alpha, x, modv, yf, yb, g, bonus, att, conv, gate, prm, bavg):
    W = RW_WIDTH
    nt = geom.nt
    lsel = lambda b, t: (layer, 0, 0)
    vec = lambda w: pl.BlockSpec((None, 1, w), lsel)
    wspec = lambda r, c: _const_spec((None, r, c), lsel)
    return pl.pallas_call(
        functools.partial(_mix_ffn_kernel, alpha=alpha),
        grid=(geom.B, nt),
        in_specs=[
            _tok_spec(geom, D_MODEL),
            _mod_spec(geom, layer, 2), _mod_spec(geom, layer, 3), _mod_spec(geom, layer, 4),
            _mod_spec(geom, layer, 5),
            _tok_spec(geom, W), _tok_spec(geom, W),
            _tok_spec(geom, W), _tok_spec(geom, W), vec(W), vec(W),
            pl.BlockSpec((W, W), lambda b, t: (0, 0)),
            _tok_spec(geom, MLA_WIDTH), _tok_spec(geom, CONV_WIDTH), _tok_spec(geom, GATE_SEG),
            wspec(MLA_WIDTH, D_MODEL), wspec(CONV_WIDTH, D_MODEL), wspec(W, D_MODEL),
            wspec(D_MODEL, D_MODEL),
            vec(D_MODEL), vec(D_MODEL),
            wspec(D_MODEL, 2 * D_FF), wspec(D_FF, D_MODEL),
            vec(D_MODEL), vec(D_MODEL),
        ],
        out_specs=_tok_spec(geom, D_MODEL),
        out_shape=jax.ShapeDtypeStruct((geom.n, D_MODEL), F32),
        compiler_params=_params(("parallel", "parallel")),
        name="mix_ffn",
    )(x, modv, modv, modv, modv, yf, yb, g, bonus, prm["rw_gn_g"], prm["rw_gn_b"], bavg,
      att, conv, gate,
      prm["w_o_attn"], prm["w_o_conv"], prm["w_o_rwkv"], prm["w_out"],
      prm["ln1_g"], prm["ln1_b"], prm["ffn_w13"], prm["ffn_w2"], prm["ln2_g"], prm["ln2_b"])


_ROPE_SWAP = np.concatenate([np.arange(8, 16), np.arange(0, 8), np.arange(24, 32), np.arange(16, 24)])


def _rope_tables(geom):
    pos = jnp.arange(geom.seq)
    row = (pos // GRID_W).astype(F32)
    col = (pos % GRID_W).astype(F32)
    axis_dim = MLA_ROPE // 2
    inv = ROPE_BASE ** (-jnp.arange(0, axis_dim, 2, dtype=F32) / axis_dim)
    ar, ac = row[:, None] * inv, col[:, None] * inv
    cr, sr, cc, sc = jnp.cos(ar), jnp.sin(ar), jnp.cos(ac), jnp.sin(ac)
    cos32 = jnp.concatenate([cr, cr, cc, cc], -1)
    sin32 = jnp.concatenate([-sr, sr, -sc, sc], -1)
    ones = jnp.ones((geom.seq, MLA_NOPE), F32)
    zpad = jnp.zeros((geom.seq, HEAD_PAD - MLA_NOPE - MLA_ROPE), F32)
    cos_l = jnp.concatenate([ones, cos32, zpad], -1)
    sin_l = jnp.concatenate([jnp.zeros_like(ones), sin32, zpad], -1)
    cos_c = jnp.ones((geom.ctx_len, HEAD_PAD), F32)
    sin_c = jnp.zeros((geom.ctx_len, HEAD_PAD), F32)
    return jnp.concatenate([cos_c, cos_l], 0), jnp.concatenate([sin_c, sin_l], 0)


def _layout_weights(w_in, w_uq, w_ukv):
    L = w_in.shape[0]
    o_q, o_kv, o_kr = MLA_Q_LORA, MLA_Q_LORA + MLA_KV_LORA, MLA_Q_LORA + MLA_KV_LORA + MLA_ROPE
    krope = w_in[:, :, o_kv:o_kr]
    zl = jnp.zeros((L, D_MODEL, MLA_NOPE), F32)
    zr = jnp.zeros((L, D_MODEL, HEAD_PAD - MLA_NOPE - MLA_ROPE), F32)
    w_in_p = jnp.concatenate(
        [w_in[:, :, :o_kv], zl, krope, zr, zl, krope[:, :, _ROPE_SWAP], zr, w_in[:, :, o_kr:]], -1).astype(BF16)

    wq = w_uq.reshape(L, MLA_Q_LORA, MLA_HEADS, MLA_NOPE + MLA_ROPE)
    q_rope = wq[..., MLA_NOPE:]
    zq = jnp.zeros((L, MLA_Q_LORA, MLA_HEADS, HEAD_PAD - MLA_NOPE - MLA_ROPE), F32)
    zn = jnp.zeros((L, MLA_Q_LORA, MLA_HEADS, MLA_NOPE), F32)
    wq_p = jnp.concatenate([wq, zq], -1).reshape(L, MLA_Q_LORA, -1).astype(BF16)
    wqs_p = jnp.concatenate([zn, q_rope[..., _ROPE_SWAP], zq], -1).reshape(L, MLA_Q_LORA, -1).astype(BF16)

    wkv = w_ukv.reshape(L, MLA_KV_LORA, MLA_HEADS, MLA_NOPE + MLA_V)
    zk = jnp.zeros((L, MLA_KV_LORA, MLA_HEADS, HEAD_PAD - MLA_NOPE), F32)
    wk_p = jnp.concatenate([wkv[..., :MLA_NOPE], zk], -1).reshape(L, MLA_KV_LORA, -1).astype(BF16)
    zv = jnp.zeros((L, MLA_KV_LORA, MLA_HEADS, HEAD_PAD - MLA_V), F32)
    wv_p = jnp.concatenate([wkv[..., MLA_NOPE:], zv], -1).reshape(L, MLA_KV_LORA, -1).astype(BF16)
    return w_in_p, wq_p, wqs_p, wk_p, wv_p


def _head_block_ones():
    idx = np.arange(RW_WIDTH) // RW_HEAD
    return jnp.asarray((idx[:, None] == idx[None, :]).astype(np.float32), dtype=BF16)


def kernel(x, c, ctx, c_ctx, mod_w, mod_b, w_in, q_norm, w_uq, kv_norm, w_ukv, w_o_attn, conv_w, w_o_conv,
           rw_mu, rw_w0, rw_w_up, rw_a0, rw_a_up, rw_g_up, rw_k_k, rw_k_a, rw_r_k, rw_gn_g, rw_gn_b,
           w_o_rwkv, w_out, ln1_g, ln1_b, ffn_w13, ffn_w2, ln2_g, ln2_b):
    B, seq, _ = x.shape
    ctx_len = ctx.shape[1]
    L = mod_w.shape[0]
    geom = _Geom(B, ctx_len, seq)
    alpha = (2.0 * L) ** 0.25

    rows = -(-(B + 1) // HALO) * HALO
    c_all = jnp.concatenate([c, c_ctx[None, :], jnp.zeros((rows - B - 1, D_MODEL), F32)], 0)
    modv = _mod_vectors(c_all, mod_w, mod_b).reshape(L, rows, 1, -1)

    w_in_p, wq_p, wqs_p, wk_p, wv_p = _layout_weights(w_in, w_uq, w_ukv)
    cos_t, sin_t = _rope_tables(geom)
    bsum = _head_block_ones()
    vec3 = lambda a: a.reshape(L, 1, -1)
    prm = dict(
        rw_mu=vec3(rw_mu), rw_w0=rw_w0, rw_w_up=rw_w_up, rw_a0=rw_a0, rw_a_up=rw_a_up, rw_g_up=rw_g_up,
        rw_k_k=vec3(rw_k_k), rw_k_a=vec3(rw_k_a), rw_r_k=vec3(rw_r_k), conv_w=conv_w,
        rw_gn_g=vec3(rw_gn_g), rw_gn_b=vec3(rw_gn_b),
        w_o_attn=w_o_attn.astype(BF16), w_o_conv=w_o_conv.astype(BF16), w_o_rwkv=w_o_rwkv.astype(BF16),
        w_out=w_out.astype(BF16), ln1_g=vec3(ln1_g), ln1_b=vec3(ln1_b),
        ffn_w13=ffn_w13.astype(BF16), ffn_w2=ffn_w2.astype(BF16), ln2_g=vec3(ln2_g), ln2_b=vec3(ln2_b),
    )
    qn, kvn = vec3(q_norm), vec3(kv_norm)

    xs = jnp.concatenate([ctx, x], axis=1).reshape(geom.n, D_MODEL)
    for l in range(L):
        mla, conv, rw, gate = _in_proj(geom, l, xs, modv, w_in_p)
        q, k, v = _mla_proj(geom, l, mla, qn, kvn, wq_p, wqs_p, wk_p, wv_p, cos_t, sin_t)
        att = _attention(geom, q, k, v)
        g, bonus, cv, gm, hm, rq, y0 = _features(geom, l, rw, conv, prm, bsum)
        yf, yb = _wkv_scan(geom, gm, hm, rq, y0)
        xs = _mix_ffn(geom, l, alpha, xs, modv, yf, yb, g, bonus, att, cv, gate, prm, bsum)
    return xs.reshape(B, geom.tt, D_MODEL)[:, ctx_len:, :]
```

```python
import functools

import numpy as np
import jax
import jax.numpy as jnp
from jax import lax
from jax.experimental import pallas as pl
from jax.experimental.pallas import tpu as pltpu

F32 = jnp.float32
BF16 = jnp.bfloat16

D_MODEL = 1024
GRID_W = 64
MLA_HEADS = 8
MLA_Q_LORA = 384
MLA_KV_LORA = 256
MLA_NOPE = 64
MLA_ROPE = 32
MLA_V = 64
MLA_WIDTH = MLA_HEADS * MLA_V
ROPE_BASE = 10000.0
CONV_WIDTH = 512
RW_HEADS = 8
RW_HEAD = 64
RW_WIDTH = RW_HEADS * RW_HEAD
RW_DECAY_LORA = 64
RW_ICLR_LORA = 64
RW_GATE_LORA = 128
RW_GN_EPS = 64e-5
RW_IN = 3 * RW_WIDTH + 2 * RW_DECAY_LORA + 2 * RW_ICLR_LORA + RW_GATE_LORA
N_BRANCH = 3
D_FF = 2816
LN_EPS = 1e-5
RMS_EPS = 1e-6

LANES = 128
HEAD_PAD = LANES
TM = 256
TM_BIG = 512
CHUNK = 64
PAIR = 2 * RW_HEAD
assert PAIR == LANES
PREP_INTERLEAVE = 4
HALO = 16
VMEM_LIMIT = 56 * 1024 * 1024

MLA_SEG = MLA_Q_LORA + MLA_KV_LORA + 2 * HEAD_PAD
CONV_SEG = 3 * CONV_WIDTH
GATE_SEG = N_BRANCH * D_MODEL
IN_SEGS = (MLA_SEG, CONV_SEG, RW_IN, GATE_SEG)
IN_OFFS = tuple(int(v) for v in np.cumsum((0,) + IN_SEGS))


def _bdot(a, b):
    return jnp.dot(a.astype(BF16), b.astype(BF16), preferred_element_type=F32)


def _split2(a):
    hi = a.astype(BF16)
    lo = (a - hi.astype(F32)).astype(BF16)
    return hi, lo


def _dot3(a, b, dims=None):
    ah, al = _split2(a)
    bh, bl = _split2(b)
    if dims is None:
        f = lambda u, v: jnp.dot(u, v, preferred_element_type=F32)
    else:
        f = lambda u, v: lax.dot_general(u, v, dims, preferred_element_type=F32)
    return f(ah, bh) + (f(ah, bl) + f(al, bh))


def _dot1(a, b, dims=None):
    a, b = a.astype(BF16), b.astype(BF16)
    if dims is None:
        return jnp.dot(a, b, preferred_element_type=F32)
    return lax.dot_general(a, b, dims, preferred_element_type=F32)


_dot_wkv = _dot1


def _dot_exact_rhs(a, b_bf16):
    a1, a2 = _split2(a)
    return jnp.dot(a1, b_bf16, preferred_element_type=F32) + jnp.dot(a2, b_bf16, preferred_element_type=F32)


def _sigmoid(x):
    return 0.5 * jnp.tanh(0.5 * x) + 0.5


def _layer_norm(x, g, b):
    mu = jnp.mean(x, -1, keepdims=True)
    xc = x - mu
    var = jnp.mean(xc * xc, -1, keepdims=True)
    return xc * lax.rsqrt(var + LN_EPS) * g + b


def _const_spec(shape, index_map):
    return pl.BlockSpec(shape, index_map, pipeline_mode=pl.Buffered(1))


def _params(sem):
    return pltpu.CompilerParams(dimension_semantics=sem, vmem_limit_bytes=VMEM_LIMIT)


def _mod_kernel(c_ref, w_ref, b_ref, o_ref):
    c = c_ref[...]
    s = c * _sigmoid(c)
    o_ref[...] = _dot3(s, w_ref[...]) + b_ref[...]


def _mod_vectors(c_all, mod_w, mod_b):
    L = mod_w.shape[0]
    R = c_all.shape[0]
    n = mod_w.shape[2] // D_MODEL
    return pl.pallas_call(
        _mod_kernel,
        grid=(L, n),
        in_specs=[
            pl.BlockSpec((R, D_MODEL), lambda l, j: (0, 0)),
            pl.BlockSpec((None, D_MODEL, D_MODEL), lambda l, j: (l, 0, j)),
            pl.BlockSpec((None, 1, D_MODEL), lambda l, j: (l, 0, j)),
        ],
        out_specs=pl.BlockSpec((None, R, D_MODEL), lambda l, j: (l, 0, j)),
        out_shape=jax.ShapeDtypeStruct((L, R, n * D_MODEL), F32),
        compiler_params=_params(("parallel", "parallel")),
        name="mod_vectors",
    )(c_all, mod_w, mod_b.reshape(L, 1, -1))


class _Geom:
    def __init__(self, B, ctx_len, seq):
        assert ctx_len % TM == 0 and seq % TM_BIG == 0 and seq % GRID_W == 0
        self.B, self.ctx_len, self.seq = B, ctx_len, seq
        self.tt = ctx_len + seq
        self.nct = ctx_len // TM
        self.nlt = seq // TM
        self.nt = self.nct + self.nlt
        self.nc = B * ctx_len
        self.n = B * self.tt
        assert self.nc % TM_BIG == 0 and self.nc % seq == 0

    def tile(self, b, t):
        return jnp.where(t < self.nct, b * self.nct + t, self.B * self.nct + b * self.nlt + (t - self.nct))

    def mod_row(self, b, t):
        return jnp.where(t < self.nct, self.B, b)


def _mod_spec(geom, layer, j, tmap=lambda t: t):
    return pl.BlockSpec((None, None, 1, D_MODEL), lambda b, t: (layer, geom.mod_row(b, tmap(t)), 0, j))


def _tok_spec(geom, width, tmap=lambda t: t):
    return pl.BlockSpec((TM, width), lambda b, t: (geom.tile(b, tmap(t)), 0))


def _in_proj_kernel(x_ref, sh_ref, sc_ref, w_ref, mla_ref, conv_ref, rw_ref, gate_ref):
    h = (x_ref[...] * (1.0 + sc_ref[...]) + sh_ref[...]).astype(BF16)
    for o_ref, lo, hi in zip((mla_ref, conv_ref, rw_ref, gate_ref), IN_OFFS[:-1], IN_OFFS[1:]):
        o_ref[...] = jnp.dot(h, w_ref[:, lo:hi], preferred_element_type=F32).astype(BF16)


def _in_proj(geom, layer, x, modv, w_in_p):
    ctx_tiles = geom.nc // TM_BIG
    per_row = geom.seq // TM_BIG

    def mod(j):
        row = lambda i: jnp.where(i < ctx_tiles, geom.B, (i - ctx_tiles) // per_row)
        return pl.BlockSpec((None, None, 1, D_MODEL), lambda i: (layer, row(i), 0, j))

    big = lambda w: pl.BlockSpec((TM_BIG, w), lambda i: (i, 0))
    return pl.pallas_call(
        _in_proj_kernel,
        grid=(geom.n // TM_BIG,),
        in_specs=[big(D_MODEL), mod(0), mod(1),
                  _const_spec((None, D_MODEL, IN_OFFS[-1]), lambda i: (layer, 0, 0))],
        out_specs=[big(w) for w in IN_SEGS],
        out_shape=[jax.ShapeDtypeStruct((geom.n, w), BF16) for w in IN_SEGS],
        compiler_params=_params(("parallel",)),
        name="in_proj",
    )(x, modv, modv, w_in_p)


def _mla_kernel(m_ref, qn_ref, kvn_ref, wq_ref, wqs_ref, wk_ref, wv_ref, cos_ref, sin_ref,
                q_ref, k_ref, v_ref):
    m = m_ref[...].astype(F32)
    cq = m[:, :MLA_Q_LORA]
    ckv = m[:, MLA_Q_LORA:MLA_Q_LORA + MLA_KV_LORA]
    kslab = m[:, MLA_Q_LORA + MLA_KV_LORA:MLA_Q_LORA + MLA_KV_LORA + HEAD_PAD]
    kslab_sw = m[:, MLA_Q_LORA + MLA_KV_LORA + HEAD_PAD:]
    cqn = (cq * lax.rsqrt(jnp.mean(cq * cq, -1, keepdims=True) + RMS_EPS) * qn_ref[...]).astype(BF16)
    ckvn = (ckv * lax.rsqrt(jnp.mean(ckv * ckv, -1, keepdims=True) + RMS_EPS) * kvn_ref[...]).astype(BF16)
    cos_t = cos_ref[...]
    sin_t = sin_ref[...]
    qa = jnp.dot(cqn, wq_ref[...], preferred_element_type=F32)
    qb = jnp.dot(cqn, wqs_ref[...], preferred_element_type=F32)
    kn = jnp.dot(ckvn, wk_ref[...], preferred_element_type=F32)
    kr = kslab * cos_t + kslab_sw * sin_t
    scale = (MLA_NOPE + MLA_ROPE) ** -0.5
    for h in range(MLA_HEADS):
        sl = slice(h * HEAD_PAD, (h + 1) * HEAD_PAD)
        q_ref[:, sl] = ((qa[:, sl] * cos_t + qb[:, sl] * sin_t) * scale).astype(BF16)
        k_ref[:, sl] = (kn[:, sl] + kr).astype(BF16)
    lane = lax.broadcasted_iota(jnp.int32, (1, MLA_HEADS * HEAD_PAD), 1) % HEAD_PAD
    v_ref[...] = (jnp.dot(ckvn, wv_ref[...], preferred_element_type=F32)
                  + (lane == MLA_V).astype(F32)).astype(BF16)


def _mla_proj(geom, layer, mla, q_norm, kv_norm, wq, wqs, wk, wv, cos_t, sin_t):
    hw = MLA_HEADS * HEAD_PAD
    lsel = lambda b, t: (layer, 0, 0)
    return pl.pallas_call(
        _mla_kernel,
        grid=(geom.B, geom.nt),
        in_specs=[
            _tok_spec(geom, MLA_SEG),
            pl.BlockSpec((None, 1, MLA_Q_LORA), lsel),
            pl.BlockSpec((None, 1, MLA_KV_LORA), lsel),
            _const_spec((None, MLA_Q_LORA, hw), lsel),
            _const_spec((None, MLA_Q_LORA, hw), lsel),
            _const_spec((None, MLA_KV_LORA, hw), lsel),
            _const_spec((None, MLA_KV_LORA, hw), lsel),
            pl.BlockSpec((TM, HEAD_PAD), lambda b, t: (t, 0)),
            pl.BlockSpec((TM, HEAD_PAD), lambda b, t: (t, 0)),
        ],
        out_specs=[_tok_spec(geom, hw)] * 3,
        out_shape=[jax.ShapeDtypeStruct((geom.n, hw), BF16)] * 3,
        compiler_params=_params(("parallel", "parallel")),
        name="mla_proj",
    )(mla, q_norm, kv_norm, wq, wqs, wk, wv, cos_t, sin_t)


HEADS_PER_STEP = 4


def _attend(q_ref, kv_refs, o_ref):
    outs = []
    for j in range(HEADS_PER_STEP):
        hs = slice(j * HEAD_PAD, (j + 1) * HEAD_PAD)
        q = q_ref[:, hs]
        ss = [lax.dot_general(q, k_ref[:, hs], (((1,), (1,)), ((), ())), preferred_element_type=F32)
              for k_ref, _ in kv_refs]
        m = functools.reduce(jnp.maximum, [jnp.max(s, -1, keepdims=True) for s in ss])
        pv = sum(jnp.dot(jnp.exp(s - m).astype(BF16), v_ref[:, hs], preferred_element_type=F32)
                 for s, (_, v_ref) in zip(ss, kv_refs))
        outs.append(pv[:, :MLA_V] / pv[:, MLA_V:MLA_V + 1])
    o_ref[...] = jnp.concatenate(outs, -1).astype(BF16)


def _attn_kernel(ql_ref, qc_ref, kc_ref, vc_ref, kl_ref, vl_ref, ol_ref, oc_ref):
    _attend(ql_ref, [(kc_ref, vc_ref), (kl_ref, vl_ref)], ol_ref)

    @pl.when(pl.program_id(2) == 0)
    def _():
        _attend(qc_ref, [(kc_ref, vc_ref)], oc_ref)


def _attention(geom, q, k, v):
    qw = HEADS_PER_STEP * HEAD_PAD
    vw = HEADS_PER_STEP * MLA_V
    per_row = geom.seq // TM_BIG
    lat0 = geom.nc // TM_BIG
    seq0 = geom.nc // geom.seq
    ctx_spec = pl.BlockSpec((geom.ctx_len, qw), lambda b, h, t: (b, h))
    lat_spec = pl.BlockSpec((geom.seq, qw), lambda b, h, t: (seq0 + b, h))
    return pl.pallas_call(
        _attn_kernel,
        grid=(geom.B, MLA_HEADS // HEADS_PER_STEP, per_row),
        in_specs=[
            pl.BlockSpec((TM_BIG, qw), lambda b, h, t: (lat0 + b * per_row + t, h)),
            ctx_spec, ctx_spec, ctx_spec, lat_spec, lat_spec,
        ],
        out_specs=[pl.BlockSpec((TM_BIG, vw), lambda b, h, t: (b * per_row + t, h)),
                   pl.BlockSpec((geom.ctx_len, vw), lambda b, h, t: (b, h))],
        out_shape=[jax.ShapeDtypeStruct((geom.B * geom.seq, MLA_WIDTH), BF16),
                   jax.ShapeDtypeStruct((geom.nc, MLA_WIDTH), BF16)],
        compiler_params=_params(("parallel", "parallel", "arbitrary")),
        name="attn",
    )(q, q, k, v, k, v)


def _shift_prev(x, halo_row):
    rows = lax.broadcasted_iota(jnp.int32, x.shape, 0)
    return jnp.where(rows == 0, halo_row, pltpu.roll(x, 1, 0))


def _shift_next(x, halo_row):
    rows = lax.broadcasted_iota(jnp.int32, x.shape, 0)
    return jnp.where(rows == x.shape[0] - 1, halo_row, pltpu.roll(x, x.shape[0] - 1, 0))


def _feat_kernel(rw_ref, rwp_ref, rwn_ref, cv_ref, cvp_ref, cvn_ref,
                 mu_ref, w0_ref, wup_ref, a0_ref, aup_ref, gup_ref, kk_ref, ka_ref, rk_ref,
                 cw_ref, bsum_ref,
                 g_o, bonus_o, conv_o, gm_o, hm_o, rq_o, y0_o, *, nct, nt):
    t = pl.program_id(1)
    first = jnp.logical_or(t == 0, t == nct).astype(F32)
    last = jnp.logical_or(t == nct - 1, t == nt - 1).astype(F32)
    keep_prev = 1.0 - first
    keep_next = 1.0 - last
    W = RW_WIDTH

    def gated(ref):
        blk = ref[...].astype(F32)
        return blk[..., 2 * CONV_WIDTH:] * blk[..., :CONV_WIDTH]
    cblk = cv_ref[...].astype(F32)
    up = cblk[:, 2 * CONV_WIDTH:] * cblk[:, :CONV_WIDTH]
    up_p = _shift_prev(up, gated(cvp_ref)[HALO - 1:HALO] * keep_prev)
    up_n = _shift_next(up, gated(cvn_ref)[0:1] * keep_next)
    cw = cw_ref[...]
    conv_o[...] = (cblk[:, CONV_WIDTH:2 * CONV_WIDTH]
                   * (up_p * cw[0:1] + up * cw[1:2] + up_n * cw[2:3])).astype(BF16)

    p = rw_ref[...].astype(F32)
    p_prev = _shift_prev(p, rwp_ref[HALO - 1:HALO, :].astype(F32) * keep_prev)
    p_next = _shift_next(p, rwn_ref[0:1, :].astype(F32) * keep_next)
    p = p + (0.5 * (p_prev + p_next) - p) * mu_ref[...]
    r = p[:, 0:W]
    k = p[:, W:2 * W]
    v = p[:, 2 * W:3 * W]
    wd = p[:, 3 * W:3 * W + 2 * RW_DECAY_LORA]
    ad = p[:, 3 * W + 2 * RW_DECAY_LORA:3 * W + 2 * RW_DECAY_LORA + 2 * RW_ICLR_LORA]
    gd = p[:, 3 * W + 2 * RW_DECAY_LORA + 2 * RW_ICLR_LORA:]

    bsum = bsum_ref[...]
    kkf = k * kk_ref[...]
    kk = kkf * lax.rsqrt(jnp.maximum(_dot_exact_rhs(kkf * kkf, bsum), 1e-24))
    g_o[...] = _dot3(_sigmoid(gd), gup_ref[...]).astype(BF16)
    bonus_o[...] = (_dot_exact_rhs(r * k * rk_ref[...], bsum) * v).astype(BF16)
    for z in range(2):
        wdz = jnp.tanh(wd[:, z * RW_DECAY_LORA:(z + 1) * RW_DECAY_LORA])
        adz = ad[:, z * RW_ICLR_LORA:(z + 1) * RW_ICLR_LORA]
        tz = w0_ref[z:z + 1, :] + _dot3(wdz, wup_ref[z])
        lw = _sigmoid(tz) * (-float(np.exp(-0.5)))
        a = _sigmoid(a0_ref[z:z + 1, :] + _dot3(adz, aup_ref[z]))
        kd = k * (1.0 + (a - 1.0) * ka_ref[...])
        _wkv_chunk_reduce_pairs(z, r, v, kk, kd, kk * a, lw, gm_o, hm_o, rq_o, y0_o)


def _features(geom, layer, rw, conv, prm, bsum):
    nt, n = geom.nt, geom.n
    blocks_per_tile = TM // HALO
    last_halo_block = n // HALO - 1

    def prev_spec(w):
        return pl.BlockSpec((HALO, w), lambda b, t: (jnp.maximum(geom.tile(b, t) * blocks_per_tile - 1, 0), 0))

    def next_spec(w):
        return pl.BlockSpec(
            (HALO, w), lambda b, t: (jnp.minimum((geom.tile(b, t) + 1) * blocks_per_tile, last_halo_block), 0))

    lsel2 = lambda b, t: (layer, 0, 0)
    lsel3 = lambda b, t: (layer, 0, 0, 0)
    W = RW_WIDTH
    tok = _tok_spec(geom, W)
    dir_spec = pl.BlockSpec((2, TM, W), lambda b, t: (0, geom.tile(b, t), 0))
    tok_shape = jax.ShapeDtypeStruct((n, W), BF16)
    dir_shape = jax.ShapeDtypeStruct((2, n, W), BF16)
    smap_rows = TM // CHUNK * PAIR
    smap_spec = pl.BlockSpec((2, smap_rows, W), lambda b, t: (0, geom.tile(b, t), 0))
    smap_shape = jax.ShapeDtypeStruct((2, n // TM * smap_rows, W), BF16)
    return pl.pallas_call(
        functools.partial(_feat_kernel, nct=geom.nct, nt=nt),
        grid=(geom.B, nt),
        in_specs=[
            _tok_spec(geom, RW_IN), prev_spec(RW_IN), next_spec(RW_IN),
            _tok_spec(geom, CONV_SEG), prev_spec(CONV_SEG), next_spec(CONV_SEG),
            pl.BlockSpec((None, 1, RW_IN), lsel2),
            pl.BlockSpec((None, 2, W), lsel2),
            pl.BlockSpec((None, 2, RW_DECAY_LORA, W), lsel3),
            pl.BlockSpec((None, 2, W), lsel2),
            pl.BlockSpec((None, 2, RW_ICLR_LORA, W), lsel3),
            pl.BlockSpec((None, RW_GATE_LORA, W), lsel2),
            pl.BlockSpec((None, 1, W), lsel2),
            pl.BlockSpec((None, 1, W), lsel2),
            pl.BlockSpec((None, 1, W), lsel2),
            pl.BlockSpec((None, 3, CONV_WIDTH), lsel2),
            pl.BlockSpec((W, W), lambda b, t: (0, 0)),
        ],
        out_specs=[tok] * 3 + [smap_spec] * 2 + [dir_spec] * 2,
        out_shape=[tok_shape] * 3 + [smap_shape] * 2 + [dir_shape] * 2,
        compiler_params=_params(("parallel", "parallel")),
        name="feat_wkv_prep",
    )(rw, rw, rw, conv, conv, conv,
      prm["rw_mu"], prm["rw_w0"], prm["rw_w_up"], prm["rw_a0"], prm["rw_a_up"], prm["rw_g_up"],
      prm["rw_k_k"], prm["rw_k_a"], prm["rw_r_k"], prm["conv_w"], bsum)


def _block_diag(x, lo, hi):
    zero = jnp.zeros_like(x)
    return jnp.concatenate([jnp.where(lo, x, zero), jnp.where(hi, x, zero)], axis=0)


def _wkv_chunk_reduce_pairs(z, r_all, v_all, kk_all, kd_all, b_all, lw_all, g_o, h_o, rq_o, y0_o):
    fwd = z == 0
    C = CHUNK
    assert C == RW_HEAD
    ri = lax.broadcasted_iota(jnp.int32, (C, C), 0)
    ci = lax.broadcasted_iota(jnp.int32, (C, C), 1)
    incl_bf = ((ci <= ri) if fwd else (ci >= ri)).astype(F32).astype(BF16)
    rp = lax.broadcasted_iota(jnp.int32, (C, PAIR), 0)
    cp = lax.broadcasted_iota(jnp.int32, (C, PAIR), 1) % C
    strict = cp < rp if fwd else cp > rp
    incl = cp <= rp if fwd else cp >= rp
    eye = (cp == rp).astype(F32)
    lane1 = lax.broadcasted_iota(jnp.int32, (1, PAIR), 1)
    lane2 = lax.broadcasted_iota(jnp.int32, (1, 2 * PAIR), 1) % PAIR
    lo1, hi1 = lane1 < RW_HEAD, lane1 >= RW_HEAD
    lo2, hi2 = lane2 < RW_HEAD, lane2 >= RW_HEAD
    lane_even = lax.broadcasted_iota(jnp.int32, (1, RW_WIDTH), 1) % PAIR < RW_HEAD
    r2 = lax.broadcasted_iota(jnp.int32, (PAIR, PAIR), 0)
    c2 = lax.broadcasted_iota(jnp.int32, (PAIR, PAIR), 1)
    same_head = (r2 < RW_HEAD) == (c2 < RW_HEAD)
    eye2 = r2 == c2
    nt_dims = (((1,), (1,)), ((), ()))
    npair = RW_WIDTH // PAIR
    pairs = [slice(p * PAIR, (p + 1) * PAIR) for p in range(npair)]

    def prologue(c):
        rows = slice(c * C, (c + 1) * C)
        lw = lw_all[rows, :]
        L = _dot_exact_rhs_lhs(incl_bf, lw)
        Ltot = L[C - 1:C, :] if fwd else L[0:1, :]
        enL = jnp.exp(-L)
        etail = jnp.exp(Ltot - L)
        kd = kd_all[rows, :]
        bb = b_all[rows, :]
        rt = r_all[rows, :] * jnp.exp(L)
        bt = (bb * enL).astype(BF16)
        kt = (kd * enL).astype(BF16)
        zero = jnp.zeros_like(bt)
        return dict(
            c=c, rows=rows, rt=rt,
            lhs=jnp.concatenate([kk_all[rows, :] * jnp.exp(L - lw), rt], axis=0).astype(BF16),
            rhs=jnp.concatenate([jnp.where(lane_even, bt, zero), jnp.where(lane_even, zero, bt),
                                 jnp.where(lane_even, kt, zero), jnp.where(lane_even, zero, kt)], axis=0),
            ktp_t=jnp.transpose(kd * etail).astype(BF16),
            btp_t=jnp.transpose(bb * etail).astype(BF16),
            ptot=jnp.exp(Ltot),
            vv=v_all[rows, :].astype(BF16))

    for c0 in range(0, TM // C, PREP_INTERLEAVE):
        chunks = [prologue(c) for c in range(c0, c0 + PREP_INTERLEAVE)]
        items = [(ch, ps) for ch in chunks for ps in pairs]
        aa = [_dot_wkv(ch["lhs"][:, ps], ch["rhs"][:, ps], nt_dims) for ch, ps in items]
        a_b = [jnp.where(strict, m[:C, :PAIR], 0.0) for m in aa]
        a_kk = [jnp.concatenate([jnp.where(strict, m[:C, PAIR:], 0.0), jnp.where(incl, m[C:, PAIR:], 0.0)], axis=0)
                .astype(BF16) for m in aa]
        a_rb = [jnp.where(incl, m[C:, :PAIR], 0.0).astype(BF16) for m in aa]
        xs = [eye - a for a in a_b]
        pw = a_b
        for _ in range(int(np.log2(C)) - 1):
            pd = [_block_diag(p.astype(BF16), lo1, hi1) for p in pw]
            pw = [_dot_wkv(p, d) for p, d in zip(pw, pd)]
            xs = [x + _dot_wkv(x, _block_diag(p.astype(BF16), lo1, hi1)) for x, p in zip(xs, pw)]
        avv = [_dot_wkv(m, _block_diag(ch["vv"][:, ps], lo1, hi1)) for m, (ch, ps) in zip(a_kk, items)]
        wu = [_dot_wkv(t, _block_diag(jnp.concatenate([ch["lhs"][:C, ps], av[:C].astype(BF16)], axis=1), lo2, hi2))
              .astype(BF16) for t, av, (ch, ps) in zip(xs, avv, items)]
        rbwu = [_dot_wkv(m, _block_diag(x, lo2, hi2)) for m, x in zip(a_rb, wu)]
        bwu = [_dot_wkv(ch["btp_t"][ps, :], x) for x, (ch, ps) in zip(wu, items)]
        kv = [_dot_wkv(ch["ktp_t"][ps, :], ch["vv"][:, ps]) for ch, ps in items]
        for av, rb, bw, kvp, (ch, ps) in zip(avv, rbwu, bwu, kv, items):
            rows = ch["rows"]
            srows = slice(ch["c"] * PAIR, (ch["c"] + 1) * PAIR)
            rq_o[z, rows, ps] = (ch["rt"][:, ps] - rb[:, :PAIR]).astype(BF16)
            y0_o[z, rows, ps] = (av[C:] - rb[:, PAIR:]).astype(BF16)
            dg = jnp.where(eye2, ch["ptot"][:, ps], 0.0)
            g_o[z, srows, ps] = jnp.where(same_head, dg - bw[:, :PAIR], 0.0).astype(BF16)
            h_o[z, srows, ps] = jnp.where(same_head, kvp - bw[:, PAIR:], 0.0).astype(BF16)


def _dot_exact_rhs_lhs(mask_bf16, x):
    x1, x2 = _split2(x)
    return jnp.dot(mask_bf16, x1, preferred_element_type=F32) + jnp.dot(mask_bf16, x2, preferred_element_type=F32)


def _wkv_scan_kernel(gf_ref, hf_ref, rqf_ref, y0f_ref, gb_ref, hb_ref, rqb_ref, y0b_ref,
                     yf_o, yb_o, state):
    C = CHUNK
    nch = TM // C

    @pl.when(pl.program_id(1) == 0)
    def _():
        state[...] = jnp.zeros_like(state)

    pairs = [slice(p * PAIR, (p + 1) * PAIR) for p in range(RW_WIDTH // PAIR)]
    dirs = ((gf_ref, hf_ref, rqf_ref, y0f_ref, yf_o), (gb_ref, hb_ref, rqb_ref, y0b_ref, yb_o))
    m = [state[0], state[1]]
    for c in range(nch):
        for z, (g_ref, h_ref, rq_ref, y0_ref, y_o) in enumerate(dirs):
            cc = c if z == 0 else nch - 1 - c
            rows = slice(cc * C, (cc + 1) * C)
            srows = slice(cc * PAIR, (cc + 1) * PAIR)
            g = g_ref[srows, :]
            rq = rq_ref[rows, :]
            mb = m[z].astype(BF16)
            ys = [jnp.dot(rq[:, ps], mb[:, ps], preferred_element_type=F32) for ps in pairs]
            ms = [jnp.dot(g[:, ps], mb[:, ps], preferred_element_type=F32) for ps in pairs]
            y_o[rows, :] = jnp.concatenate(ys, -1) + y0_ref[rows, :]
            m[z] = jnp.concatenate(ms, -1) + h_ref[srows, :]
    state[0] = m[0]
    state[1] = m[1]


def _wkv_scan(geom, g, h, rq, y0):
    W = RW_WIDTH
    nt, nct = geom.nt, geom.nct

    def fwd_tile(b, t):
        return (0, geom.tile(b, t), 0)

    def bwd_tile(b, t):
        return (1, geom.tile(b, jnp.where(t < nct, nct - 1 - t, nt - 1 - (t - nct))), 0)

    fspec = pl.BlockSpec((None, TM, W), fwd_tile)
    bspec = pl.BlockSpec((None, TM, W), bwd_tile)
    srows = TM // CHUNK * PAIR
    fsspec = pl.BlockSpec((None, srows, W), fwd_tile)
    bsspec = pl.BlockSpec((None, srows, W), bwd_tile)
    return pl.pallas_call(
        _wkv_scan_kernel,
        grid=(geom.B, nt),
        in_specs=[fsspec, fsspec, fspec, fspec, bsspec, bsspec, bspec, bspec],
        out_specs=[pl.BlockSpec((TM, W), lambda b, t: fwd_tile(b, t)[1:]),
                   pl.BlockSpec((TM, W), lambda b, t: bwd_tile(b, t)[1:])],
        out_shape=[jax.ShapeDtypeStruct((geom.n, W), F32)] * 2,
        scratch_shapes=[pltpu.VMEM((2, PAIR, W), F32)],
        compiler_params=_params(("parallel", "arbitrary")),
        name="wkv_scan",
    )(g, h, rq, y0, g, h, rq, y0)


def _mix_ffn_kernel(x_ref, g1_ref, sh2_ref, sc2_ref, g2_ref,
                    yf_ref, yb_ref, gg_ref, bonus_ref, gng_ref, gnb_ref, bavg_ref,
                    attc_ref, attl_ref, conv_ref, gate_ref,
                    woa_ref, woc_ref, wor_ref, wout_ref,
                    l1g_ref, l1b_ref, w13_ref, w2_ref, l2g_ref, l2b_ref,
                    o_ref, *, alpha, nct, t0):
    y = yf_ref[...] + yb_ref[...]
    bavg = bavg_ref[...]
    mu = _dot_exact_rhs(y, bavg) * (1.0 / RW_HEAD)
    yc = y - mu
    var = _dot_exact_rhs(yc * yc, bavg) * (1.0 / RW_HEAD)
    yn = yc * lax.rsqrt(var + RW_GN_EPS) * gng_ref[...] + gnb_ref[...]
    rwo = (yn + bonus_ref[...]) * gg_ref[...]

    gl = gate_ref[...].astype(F32)
    att = jnp.where(pl.program_id(1) + t0 < nct, attc_ref[...], attl_ref[...])
    merged = (_sigmoid(gl[:, 0:D_MODEL]) * _bdot(att, woa_ref[...])
              + _sigmoid(gl[:, D_MODEL:2 * D_MODEL]) * _bdot(conv_ref[...], woc_ref[...])
              + _sigmoid(gl[:, 2 * D_MODEL:]) * _bdot(rwo, wor_ref[...]))
    o = _bdot(merged, wout_ref[...])
    x = x_ref[...]
    x1 = _layer_norm(alpha * x + g1_ref[...] * o, l1g_ref[...], l1b_ref[...])

    hmod = x1 * (1.0 + sc2_ref[...]) + sh2_ref[...]
    ug = _bdot(hmod, w13_ref[...])
    u = ug[:, :D_FF]
    f = _bdot(u * _sigmoid(u) * ug[:, D_FF:], w2_ref[...])
    o_ref[...] = _layer_norm(alpha * x1 + g2_ref[...] * f, l2g_ref[...], l2b_ref[...])


def _mix_ffn(geom, layer, alpha, x, modv, yf, yb, g, bonus, att_lat, att_ctx, conv, gate, prm, bavg, latent_only):
    W = RW_WIDTH
    nct, nlt = geom.nct, geom.nlt
    t0 = nct if latent_only else 0
    lsel = lambda b, t: (layer, 0, 0)
    vec = lambda w: pl.BlockSpec((None, 1, w), lsel)
    wspec = lambda r, c: _const_spec((None, r, c), lsel)
    cur = lambda t: t + t0
    tok = lambda w: _tok_spec(geom, w, cur)
    mod = lambda j: _mod_spec(geom, layer, j, cur)
    attc_spec = pl.BlockSpec((TM, MLA_WIDTH), lambda b, t: (b * nct + jnp.minimum(cur(t), nct - 1), 0))
    attl_spec = pl.BlockSpec((TM, MLA_WIDTH), lambda b, t: (b * nlt + jnp.maximum(cur(t) - nct, 0), 0))
    if latent_only:
        out_spec = pl.BlockSpec((TM, D_MODEL), lambda b, t: (b * nlt + t, 0))
        out_rows = geom.B * geom.seq
    else:
        out_spec = tok(D_MODEL)
        out_rows = geom.n
    return pl.pallas_call(
        functools.partial(_mix_ffn_kernel, alpha=alpha, nct=nct, t0=t0),
        grid=(geom.B, geom.nt - t0),
        in_specs=[
            tok(D_MODEL),
            mod(2), mod(3), mod(4), mod(5),
            tok(W), tok(W),
            tok(W), tok(W), vec(W), vec(W),
            pl.BlockSpec((W, W), lambda b, t: (0, 0)),
            attc_spec, attl_spec, tok(CONV_WIDTH), tok(GATE_SEG),
            wspec(MLA_WIDTH, D_MODEL), wspec(CONV_WIDTH, D_MODEL), wspec(W, D_MODEL),
            wspec(D_MODEL, D_MODEL),
            vec(D_MODEL), vec(D_MODEL),
            wspec(D_MODEL, 2 * D_FF), wspec(D_FF, D_MODEL),
            vec(D_MODEL), vec(D_MODEL),
        ],
        out_specs=out_spec,
        out_shape=jax.ShapeDtypeStruct((out_rows, D_MODEL), F32),
        compiler_params=_params(("parallel", "parallel")),
        name="mix_ffn",
    )(x, modv, modv, modv, modv, yf, yb, g, bonus, prm["rw_gn_g"], prm["rw_gn_b"], bavg,
      att_ctx, att_lat, conv, gate,
      prm["w_o_attn"], prm["w_o_conv"], prm["w_o_rwkv"], prm["w_out"],
      prm["ln1_g"], prm["ln1_b"], prm["ffn_w13"], prm["ffn_w2"], prm["ln2_g"], prm["ln2_b"])


_ROPE_SWAP = np.concatenate([np.arange(8, 16), np.arange(0, 8), np.arange(24, 32), np.arange(16, 24)])


def _rope_tables(geom):
    pos = jnp.arange(geom.seq)
    row = (pos // GRID_W).astype(F32)
    col = (pos % GRID_W).astype(F32)
    axis_dim = MLA_ROPE // 2
    inv = ROPE_BASE ** (-jnp.arange(0, axis_dim, 2, dtype=F32) / axis_dim)
    ar, ac = row[:, None] * inv, col[:, None] * inv
    cr, sr, cc, sc = jnp.cos(ar), jnp.sin(ar), jnp.cos(ac), jnp.sin(ac)
    cos32 = jnp.concatenate([cr, cr, cc, cc], -1)
    sin32 = jnp.concatenate([-sr, sr, -sc, sc], -1)
    ones = jnp.ones((geom.seq, MLA_NOPE), F32)
    zpad = jnp.zeros((geom.seq, HEAD_PAD - MLA_NOPE - MLA_ROPE), F32)
    cos_l = jnp.concatenate([ones, cos32, zpad], -1)
    sin_l = jnp.concatenate([jnp.zeros_like(ones), sin32, zpad], -1)
    cos_c = jnp.ones((geom.ctx_len, HEAD_PAD), F32)
    sin_c = jnp.zeros((geom.ctx_len, HEAD_PAD), F32)
    return jnp.concatenate([cos_c, cos_l], 0), jnp.concatenate([sin_c, sin_l], 0)


def _layout_weights(w_in, w_uq, w_ukv):
    L = w_in.shape[0]
    o_q, o_kv, o_kr = MLA_Q_LORA, MLA_Q_LORA + MLA_KV_LORA, MLA_Q_LORA + MLA_KV_LORA + MLA_ROPE
    krope = w_in[:, :, o_kv:o_kr]
    zl = jnp.zeros((L, D_MODEL, MLA_NOPE), F32)
    zr = jnp.zeros((L, D_MODEL, HEAD_PAD - MLA_NOPE - MLA_ROPE), F32)
    w_in_p = jnp.concatenate(
        [w_in[:, :, :o_kv], zl, krope, zr, zl, krope[:, :, _ROPE_SWAP], zr, w_in[:, :, o_kr:]], -1).astype(BF16)

    wq = w_uq.reshape(L, MLA_Q_LORA, MLA_HEADS, MLA_NOPE + MLA_ROPE)
    q_rope = wq[..., MLA_NOPE:]
    zq = jnp.zeros((L, MLA_Q_LORA, MLA_HEADS, HEAD_PAD - MLA_NOPE - MLA_ROPE), F32)
    zn = jnp.zeros((L, MLA_Q_LORA, MLA_HEADS, MLA_NOPE), F32)
    wq_p = jnp.concatenate([wq, zq], -1).reshape(L, MLA_Q_LORA, -1).astype(BF16)
    wqs_p = jnp.concatenate([zn, q_rope[..., _ROPE_SWAP], zq], -1).reshape(L, MLA_Q_LORA, -1).astype(BF16)

    wkv = w_ukv.reshape(L, MLA_KV_LORA, MLA_HEADS, MLA_NOPE + MLA_V)
    zk = jnp.zeros((L, MLA_KV_LORA, MLA_HEADS, HEAD_PAD - MLA_NOPE), F32)
    wk_p = jnp.concatenate([wkv[..., :MLA_NOPE], zk], -1).reshape(L, MLA_KV_LORA, -1).astype(BF16)
    zv = jnp.zeros((L, MLA_KV_LORA, MLA_HEADS, HEAD_PAD - MLA_V), F32)
    wv_p = jnp.concatenate([wkv[..., MLA_NOPE:], zv], -1).reshape(L, MLA_KV_LORA, -1).astype(BF16)
    return w_in_p, wq_p, wqs_p, wk_p, wv_p


def _head_block_ones():
    idx = np.arange(RW_WIDTH) // RW_HEAD
    return jnp.asarray((idx[:, None] == idx[None, :]).astype(np.float32), dtype=BF16)


def kernel(x, c, ctx, c_ctx, mod_w, mod_b, w_in, q_norm, w_uq, kv_norm, w_ukv, w_o_attn, conv_w, w_o_conv,
           rw_mu, rw_w0, rw_w_up, rw_a0, rw_a_up, rw_g_up, rw_k_k, rw_k_a, rw_r_k, rw_gn_g, rw_gn_b,
           w_o_rwkv, w_out, ln1_g, ln1_b, ffn_w13, ffn_w2, ln2_g, ln2_b):
    B, seq, _ = x.shape
    ctx_len = ctx.shape[1]
    L = mod_w.shape[0]
    geom = _Geom(B, ctx_len, seq)
    alpha = (2.0 * L) ** 0.25

    rows = -(-(B + 1) // HALO) * HALO
    c_all = jnp.concatenate([c, c_ctx[None, :], jnp.zeros((rows - B - 1, D_MODEL), F32)], 0)
    modv = _mod_vectors(c_all, mod_w, mod_b).reshape(L, rows, 1, -1)

    w_in_p, wq_p, wqs_p, wk_p, wv_p = _layout_weights(w_in, w_uq, w_ukv)
    cos_t, sin_t = _rope_tables(geom)
    bsum = _head_block_ones()
    vec3 = lambda a: a.reshape(L, 1, -1)
    prm = dict(
        rw_mu=vec3(rw_mu), rw_w0=rw_w0, rw_w_up=rw_w_up, rw_a0=rw_a0, rw_a_up=rw_a_up, rw_g_up=rw_g_up,
        rw_k_k=vec3(rw_k_k), rw_k_a=vec3(rw_k_a), rw_r_k=vec3(rw_r_k), conv_w=conv_w,
        rw_gn_g=vec3(rw_gn_g), rw_gn_b=vec3(rw_gn_b),
        w_o_attn=w_o_attn.astype(BF16), w_o_conv=w_o_conv.astype(BF16), w_o_rwkv=w_o_rwkv.astype(BF16),
        w_out=w_out.astype(BF16), ln1_g=vec3(ln1_g), ln1_b=vec3(ln1_b),
        ffn_w13=ffn_w13.astype(BF16), ffn_w2=ffn_w2.astype(BF16), ln2_g=vec3(ln2_g), ln2_b=vec3(ln2_b),
    )
    qn, kvn = vec3(q_norm), vec3(kv_norm)

    xs = jnp.concatenate([ctx.reshape(geom.nc, D_MODEL), x.reshape(B * seq, D_MODEL)], axis=0)
    for l in range(L):
        mla, conv, rw, gate = _in_proj(geom, l, xs, modv, w_in_p)
        q, k, v = _mla_proj(geom, l, mla, qn, kvn, wq_p, wqs_p, wk_p, wv_p, cos_t, sin_t)
        att_lat, att_ctx = _attention(geom, q, k, v)
        g, bonus, cv, gm, hm, rq, y0 = _features(geom, l, rw, conv, prm, bsum)
        yf, yb = _wkv_scan(geom, gm, hm, rq, y0)
        xs = _mix_ffn(geom, l, alpha, xs, modv, yf, yb, g, bonus, att_lat, att_ctx, cv, gate, prm, bsum,
                      latent_only=(l == L - 1))
    return xs.reshape(B, seq, D_MODEL)
```

```python
import functools

import numpy as np
import jax
import jax.numpy as jnp
from jax import lax
from jax.experimental import pallas as pl
from jax.experimental.pallas import tpu as pltpu

F32 = jnp.float32
BF16 = jnp.bfloat16

D_MODEL = 1024
GRID_W = 64
MLA_HEADS = 8
MLA_Q_LORA = 384
MLA_KV_LORA = 256
MLA_NOPE = 64
MLA_ROPE = 32
MLA_V = 64
MLA_WIDTH = MLA_HEADS * MLA_V
ROPE_BASE = 10000.0
CONV_WIDTH = 512
RW_HEADS = 8
RW_HEAD = 64
RW_WIDTH = RW_HEADS * RW_HEAD
RW_DECAY_LORA = 64
RW_ICLR_LORA = 64
RW_GATE_LORA = 128
RW_GN_EPS = 64e-5
RW_IN = 3 * RW_WIDTH + 2 * RW_DECAY_LORA + 2 * RW_ICLR_LORA + RW_GATE_LORA
N_BRANCH = 3
D_FF = 2816
LN_EPS = 1e-5
RMS_EPS = 1e-6

LANES = 128
HEAD_PAD = LANES
TM = 256
TM_BIG = 512
CHUNK = 64
PAIR = 2 * RW_HEAD
assert PAIR == LANES
PREP_INTERLEAVE = 4
HALO = 16
VMEM_LIMIT = 56 * 1024 * 1024

MLA_SEG = MLA_Q_LORA + MLA_KV_LORA + 2 * HEAD_PAD
CONV_SEG = 3 * CONV_WIDTH
GATE_SEG = N_BRANCH * D_MODEL
IN_SEGS = (MLA_SEG, CONV_SEG, RW_IN, GATE_SEG)
IN_OFFS = tuple(int(v) for v in np.cumsum((0,) + IN_SEGS))


def _bdot(a, b):
    return jnp.dot(a.astype(BF16), b.astype(BF16), preferred_element_type=F32)


def _split2(a):
    hi = a.astype(BF16)
    lo = (a - hi.astype(F32)).astype(BF16)
    return hi, lo


def _dot3(a, b, dims=None):
    ah, al = _split2(a)
    bh, bl = _split2(b)
    if dims is None:
        f = lambda u, v: jnp.dot(u, v, preferred_element_type=F32)
    else:
        f = lambda u, v: lax.dot_general(u, v, dims, preferred_element_type=F32)
    return f(ah, bh) + (f(ah, bl) + f(al, bh))


def _dot1(a, b, dims=None):
    a, b = a.astype(BF16), b.astype(BF16)
    if dims is None:
        return jnp.dot(a, b, preferred_element_type=F32)
    return lax.dot_general(a, b, dims, preferred_element_type=F32)


_dot_wkv = _dot1


def _dot_exact_rhs(a, b_bf16):
    a1, a2 = _split2(a)
    return jnp.dot(a1, b_bf16, preferred_element_type=F32) + jnp.dot(a2, b_bf16, preferred_element_type=F32)


def _sigmoid(x):
    return 0.5 * jnp.tanh(0.5 * x) + 0.5


def _layer_norm(x, g, b):
    mu = jnp.mean(x, -1, keepdims=True)
    xc = x - mu
    var = jnp.mean(xc * xc, -1, keepdims=True)
    return xc * lax.rsqrt(var + LN_EPS) * g + b


def _const_spec(shape, index_map):
    return pl.BlockSpec(shape, index_map, pipeline_mode=pl.Buffered(1))


def _params(sem):
    return pltpu.CompilerParams(dimension_semantics=sem, vmem_limit_bytes=VMEM_LIMIT)


def _mod_kernel(c_ref, w_ref, b_ref, o_ref):
    c = c_ref[...]
    s = c * _sigmoid(c)
    o_ref[...] = _dot3(s, w_ref[...]) + b_ref[...]


def _mod_vectors(c_all, mod_w, mod_b):
    L = mod_w.shape[0]
    R = c_all.shape[0]
    n = mod_w.shape[2] // D_MODEL
    return pl.pallas_call(
        _mod_kernel,
        grid=(L, n),
        in_specs=[
            pl.BlockSpec((R, D_MODEL), lambda l, j: (0, 0)),
            pl.BlockSpec((None, D_MODEL, D_MODEL), lambda l, j: (l, 0, j)),
            pl.BlockSpec((None, 1, D_MODEL), lambda l, j: (l, 0, j)),
        ],
        out_specs=pl.BlockSpec((None, R, D_MODEL), lambda l, j: (l, 0, j)),
        out_shape=jax.ShapeDtypeStruct((L, R, n * D_MODEL), F32),
        compiler_params=_params(("parallel", "parallel")),
        name="mod_vectors",
    )(c_all, mod_w, mod_b.reshape(L, 1, -1))


class _Geom:
    def __init__(self, B, ctx_len, seq):
        assert ctx_len % TM == 0 and seq % TM_BIG == 0 and seq % GRID_W == 0
        self.B, self.ctx_len, self.seq = B, ctx_len, seq
        self.tt = ctx_len + seq
        self.nct = ctx_len // TM
        self.nlt = seq // TM
        self.nt = self.nct + self.nlt
        self.nc = B * ctx_len
        self.n = B * self.tt
        assert self.nc % TM_BIG == 0 and self.nc % seq == 0

    def tile(self, b, t):
        return jnp.where(t < self.nct, b * self.nct + t, self.B * self.nct + b * self.nlt + (t - self.nct))

    def mod_row(self, b, t):
        return jnp.where(t < self.nct, self.B, b)


def _mod_spec(geom, layer, j, tmap=lambda t: t):
    return pl.BlockSpec((None, None, 1, D_MODEL), lambda b, t: (layer, geom.mod_row(b, tmap(t)), 0, j))


def _tok_spec(geom, width, tmap=lambda t: t):
    return pl.BlockSpec((TM, width), lambda b, t: (geom.tile(b, tmap(t)), 0))


def _in_proj_kernel(x_ref, sh_ref, sc_ref, w_ref, mla_ref, conv_ref, rw_ref, gate_ref):
    h = (x_ref[...] * (1.0 + sc_ref[...]) + sh_ref[...]).astype(BF16)
    for o_ref, lo, hi in zip((mla_ref, conv_ref, rw_ref, gate_ref), IN_OFFS[:-1], IN_OFFS[1:]):
        o_ref[...] = jnp.dot(h, w_ref[:, lo:hi], preferred_element_type=F32).astype(BF16)


def _in_proj(geom, layer, x, modv, w_in_p):
    ctx_tiles = geom.nc // TM_BIG
    per_row = geom.seq // TM_BIG

    def mod(j):
        row = lambda i: jnp.where(i < ctx_tiles, geom.B, (i - ctx_tiles) // per_row)
        return pl.BlockSpec((None, None, 1, D_MODEL), lambda i: (layer, row(i), 0, j))

    big = lambda w: pl.BlockSpec((TM_BIG, w), lambda i: (i, 0))
    return pl.pallas_call(
        _in_proj_kernel,
        grid=(geom.n // TM_BIG,),
        in_specs=[big(D_MODEL), mod(0), mod(1),
                  _const_spec((None, D_MODEL, IN_OFFS[-1]), lambda i: (layer, 0, 0))],
        out_specs=[big(w) for w in IN_SEGS],
        out_shape=[jax.ShapeDtypeStruct((geom.n, w), BF16) for w in IN_SEGS],
        compiler_params=_params(("parallel",)),
        name="in_proj",
    )(x, modv, modv, w_in_p)


def _mla_kernel(m_ref, qn_ref, kvn_ref, wq_ref, wqs_ref, wk_ref, wv_ref, cos_ref, sin_ref,
                q_ref, k_ref, v_ref):
    m = m_ref[...].astype(F32)
    cq = m[:, :MLA_Q_LORA]
    ckv = m[:, MLA_Q_LORA:MLA_Q_LORA + MLA_KV_LORA]
    kslab = m[:, MLA_Q_LORA + MLA_KV_LORA:MLA_Q_LORA + MLA_KV_LORA + HEAD_PAD]
    kslab_sw = m[:, MLA_Q_LORA + MLA_KV_LORA + HEAD_PAD:]
    cqn = (cq * lax.rsqrt(jnp.mean(cq * cq, -1, keepdims=True) + RMS_EPS) * qn_ref[...]).astype(BF16)
    ckvn = (ckv * lax.rsqrt(jnp.mean(ckv * ckv, -1, keepdims=True) + RMS_EPS) * kvn_ref[...]).astype(BF16)
    cos_t = cos_ref[...]
    sin_t = sin_ref[...]
    qa = jnp.dot(cqn, wq_ref[...], preferred_element_type=F32)
    qb = jnp.dot(cqn, wqs_ref[...], preferred_element_type=F32)
    kn = jnp.dot(ckvn, wk_ref[...], preferred_element_type=F32)
    kr = kslab * cos_t + kslab_sw * sin_t
    scale = (MLA_NOPE + MLA_ROPE) ** -0.5
    for h in range(MLA_HEADS):
        sl = slice(h * HEAD_PAD, (h + 1) * HEAD_PAD)
        q_ref[:, sl] = ((qa[:, sl] * cos_t + qb[:, sl] * sin_t) * scale).astype(BF16)
        k_ref[:, sl] = (kn[:, sl] + kr).astype(BF16)
    lane = lax.broadcasted_iota(jnp.int32, (1, MLA_HEADS * HEAD_PAD), 1) % HEAD_PAD
    v_ref[...] = (jnp.dot(ckvn, wv_ref[...], preferred_element_type=F32)
                  + (lane == MLA_V).astype(F32)).astype(BF16)


def _mla_proj(geom, layer, mla, q_norm, kv_norm, wq, wqs, wk, wv, cos_t, sin_t):
    hw = MLA_HEADS * HEAD_PAD
    lsel = lambda b, t: (layer, 0, 0)
    return pl.pallas_call(
        _mla_kernel,
        grid=(geom.B, geom.nt),
        in_specs=[
            _tok_spec(geom, MLA_SEG),
            pl.BlockSpec((None, 1, MLA_Q_LORA), lsel),
            pl.BlockSpec((None, 1, MLA_KV_LORA), lsel),
            _const_spec((None, MLA_Q_LORA, hw), lsel),
            _const_spec((None, MLA_Q_LORA, hw), lsel),
            _const_spec((None, MLA_KV_LORA, hw), lsel),
            _const_spec((None, MLA_KV_LORA, hw), lsel),
            pl.BlockSpec((TM, HEAD_PAD), lambda b, t: (t, 0)),
            pl.BlockSpec((TM, HEAD_PAD), lambda b, t: (t, 0)),
        ],
        out_specs=[_tok_spec(geom, hw)] * 3,
        out_shape=[jax.ShapeDtypeStruct((geom.n, hw), BF16)] * 3,
        compiler_params=_params(("parallel", "parallel")),
        name="mla_proj",
    )(mla, q_norm, kv_norm, wq, wqs, wk, wv, cos_t, sin_t)


HEADS_PER_STEP = 4


def _attend(q_ref, kv_refs, o_ref):
    heads = [slice(j * HEAD_PAD, (j + 1) * HEAD_PAD) for j in range(HEADS_PER_STEP)]

    def scores(hs):
        q = q_ref[:, hs]
        return [lax.dot_general(q, k_ref[:, hs], (((1,), (1,)), ((), ())), preferred_element_type=F32)
                for k_ref, _ in kv_refs]

    def probs(ss):
        m = functools.reduce(jnp.maximum, [jnp.max(s, -1, keepdims=True) for s in ss])
        return [jnp.exp(s - m).astype(BF16) for s in ss]

    def weighted(ps, hs):
        pv = sum(jnp.dot(p, v_ref[:, hs], preferred_element_type=F32) for p, (_, v_ref) in zip(ps, kv_refs))
        return pv[:, :MLA_V] / pv[:, MLA_V:MLA_V + 1]

    n = len(heads)
    ss = [scores(heads[0]), scores(heads[1])] + [None] * (n - 2)
    ps = [probs(ss[0])] + [None] * (n - 1)
    outs = []
    for j in range(n):
        if j + 2 < n:
            ss[j + 2] = scores(heads[j + 2])
        if j + 1 < n:
            ps[j + 1] = probs(ss[j + 1])
        outs.append(weighted(ps[j], heads[j]))
    o_ref[...] = jnp.concatenate(outs, -1).astype(BF16)


def _attn_kernel(ql_ref, qc_ref, kc_ref, vc_ref, kl_ref, vl_ref, ol_ref, oc_ref):
    _attend(ql_ref, [(kc_ref, vc_ref), (kl_ref, vl_ref)], ol_ref)

    @pl.when(pl.program_id(2) == 0)
    def _():
        _attend(qc_ref, [(kc_ref, vc_ref)], oc_ref)


def _attention(geom, q, k, v):
    qw = HEADS_PER_STEP * HEAD_PAD
    vw = HEADS_PER_STEP * MLA_V
    per_row = geom.seq // TM_BIG
    lat0 = geom.nc // TM_BIG
    seq0 = geom.nc // geom.seq
    ctx_spec = pl.BlockSpec((geom.ctx_len, qw), lambda b, h, t: (b, h))
    lat_spec = pl.BlockSpec((geom.seq, qw), lambda b, h, t: (seq0 + b, h))
    return pl.pallas_call(
        _attn_kernel,
        grid=(geom.B, MLA_HEADS // HEADS_PER_STEP, per_row),
        in_specs=[
            pl.BlockSpec((TM_BIG, qw), lambda b, h, t: (lat0 + b * per_row + t, h)),
            ctx_spec, ctx_spec, ctx_spec, lat_spec, lat_spec,
        ],
        out_specs=[pl.BlockSpec((TM_BIG, vw), lambda b, h, t: (b * per_row + t, h)),
                   pl.BlockSpec((geom.ctx_len, vw), lambda b, h, t: (b, h))],
        out_shape=[jax.ShapeDtypeStruct((geom.B * geom.seq, MLA_WIDTH), BF16),
                   jax.ShapeDtypeStruct((geom.nc, MLA_WIDTH), BF16)],
        compiler_params=_params(("parallel", "parallel", "arbitrary")),
        name="attn",
    )(q, q, k, v, k, v)


def _shift_prev(x, halo_row):
    rows = lax.broadcasted_iota(jnp.int32, x.shape, 0)
    return jnp.where(rows == 0, halo_row, pltpu.roll(x, 1, 0))


def _shift_next(x, halo_row):
    rows = lax.broadcasted_iota(jnp.int32, x.shape, 0)
    return jnp.where(rows == x.shape[0] - 1, halo_row, pltpu.roll(x, x.shape[0] - 1, 0))


def _feat_kernel(rw_ref, rwp_ref, rwn_ref, cv_ref, cvp_ref, cvn_ref,
                 mu_ref, w0_ref, wup_ref, a0_ref, aup_ref, gup_ref, kk_ref, ka_ref, rk_ref,
                 cw_ref, bsum_ref,
                 g_o, bonus_o, conv_o, gm_o, hm_o, rq_o, y0_o, *, nct, nt):
    t = pl.program_id(1)
    first = jnp.logical_or(t == 0, t == nct).astype(F32)
    last = jnp.logical_or(t == nct - 1, t == nt - 1).astype(F32)
    keep_prev = 1.0 - first
    keep_next = 1.0 - last
    W = RW_WIDTH

    def gated(ref):
        blk = ref[...].astype(F32)
        return blk[..., 2 * CONV_WIDTH:] * blk[..., :CONV_WIDTH]
    cblk = cv_ref[...].astype(F32)
    up = cblk[:, 2 * CONV_WIDTH:] * cblk[:, :CONV_WIDTH]
    up_p = _shift_prev(up, gated(cvp_ref)[HALO - 1:HALO] * keep_prev)
    up_n = _shift_next(up, gated(cvn_ref)[0:1] * keep_next)
    cw = cw_ref[...]
    conv_o[...] = (cblk[:, CONV_WIDTH:2 * CONV_WIDTH]
                   * (up_p * cw[0:1] + up * cw[1:2] + up_n * cw[2:3])).astype(BF16)

    p = rw_ref[...].astype(F32)
    p_prev = _shift_prev(p, rwp_ref[HALO - 1:HALO, :].astype(F32) * keep_prev)
    p_next = _shift_next(p, rwn_ref[0:1, :].astype(F32) * keep_next)
    p = p + (0.5 * (p_prev + p_next) - p) * mu_ref[...]
    r = p[:, 0:W]
    k = p[:, W:2 * W]
    v = p[:, 2 * W:3 * W]
    wd = p[:, 3 * W:3 * W + 2 * RW_DECAY_LORA]
    ad = p[:, 3 * W + 2 * RW_DECAY_LORA:3 * W + 2 * RW_DECAY_LORA + 2 * RW_ICLR_LORA]
    gd = p[:, 3 * W + 2 * RW_DECAY_LORA + 2 * RW_ICLR_LORA:]

    bsum = bsum_ref[...]
    kkf = k * kk_ref[...]
    kk = kkf * lax.rsqrt(jnp.maximum(_dot_exact_rhs(kkf * kkf, bsum), 1e-24))
    g_o[...] = _dot3(_sigmoid(gd), gup_ref[...]).astype(BF16)
    bonus_o[...] = (_dot_exact_rhs(r * k * rk_ref[...], bsum) * v).astype(BF16)
    for z in range(2):
        wdz = jnp.tanh(wd[:, z * RW_DECAY_LORA:(z + 1) * RW_DECAY_LORA])
        adz = ad[:, z * RW_ICLR_LORA:(z + 1) * RW_ICLR_LORA]
        tz = w0_ref[z:z + 1, :] + _dot3(wdz, wup_ref[z])
        lw = _sigmoid(tz) * (-float(np.exp(-0.5)))
        a = _sigmoid(a0_ref[z:z + 1, :] + _dot3(adz, aup_ref[z]))
        kd = k * (1.0 + (a - 1.0) * ka_ref[...])
        _wkv_chunk_reduce_pairs(z, r, v, kk, kd, kk * a, lw, gm_o, hm_o, rq_o, y0_o)


def _features(geom, layer, rw, conv, prm, bsum):
    nt, n = geom.nt, geom.n
    blocks_per_tile = TM // HALO
    last_halo_block = n // HALO - 1

    def prev_spec(w):
        return pl.BlockSpec((HALO, w), lambda b, t: (jnp.maximum(geom.tile(b, t) * blocks_per_tile - 1, 0), 0))

    def next_spec(w):
        return pl.BlockSpec(
            (HALO, w), lambda b, t: (jnp.minimum((geom.tile(b, t) + 1) * blocks_per_tile, last_halo_block), 0))

    lsel2 = lambda b, t: (layer, 0, 0)
    lsel3 = lambda b, t: (layer, 0, 0, 0)
    W = RW_WIDTH
    tok = _tok_spec(geom, W)
    dir_spec = pl.BlockSpec((2, TM, W), lambda b, t: (0, geom.tile(b, t), 0))
    tok_shape = jax.ShapeDtypeStruct((n, W), BF16)
    dir_shape = jax.ShapeDtypeStruct((2, n, W), BF16)
    return pl.pallas_call(
        functools.partial(_feat_kernel, nct=geom.nct, nt=nt),
        grid=(geom.B, nt),
        in_specs=[
            _tok_spec(geom, RW_IN), prev_spec(RW_IN), next_spec(RW_IN),
            _tok_spec(geom, CONV_SEG), prev_spec(CONV_SEG), next_spec(CONV_SEG),
            pl.BlockSpec((None, 1, RW_IN), lsel2),
            pl.BlockSpec((None, 2, W), lsel2),
            pl.BlockSpec((None, 2, RW_DECAY_LORA, W), lsel3),
            pl.BlockSpec((None, 2, W), lsel2),
            pl.BlockSpec((None, 2, RW_ICLR_LORA, W), lsel3),
            pl.BlockSpec((None, RW_GATE_LORA, W), lsel2),
            pl.BlockSpec((None, 1, W), lsel2),
            pl.BlockSpec((None, 1, W), lsel2),
            pl.BlockSpec((None, 1, W), lsel2),
            pl.BlockSpec((None, 3, CONV_WIDTH), lsel2),
            pl.BlockSpec((W, W), lambda b, t: (0, 0)),
        ],
        out_specs=[tok] * 3 + [dir_spec] * 4,
        out_shape=[tok_shape] * 3 + [dir_shape] * 4,
        compiler_params=_params(("parallel", "parallel")),
        name="feat_wkv_prep",
    )(rw, rw, rw, conv, conv, conv,
      prm["rw_mu"], prm["rw_w0"], prm["rw_w_up"], prm["rw_a0"], prm["rw_a_up"], prm["rw_g_up"],
      prm["rw_k_k"], prm["rw_k_a"], prm["rw_r_k"], prm["conv_w"], bsum)


def _block_diag(x, lo, hi):
    zero = jnp.zeros_like(x)
    return jnp.concatenate([jnp.where(lo, x, zero), jnp.where(hi, x, zero)], axis=0)


def _wkv_chunk_reduce_pairs(z, r_all, v_all, kk_all, kd_all, b_all, lw_all, g_o, h_o, rq_o, y0_o):
    fwd = z == 0
    C = CHUNK
    assert C == RW_HEAD
    ri = lax.broadcasted_iota(jnp.int32, (C, C), 0)
    ci = lax.broadcasted_iota(jnp.int32, (C, C), 1)
    incl_bf = ((ci <= ri) if fwd else (ci >= ri)).astype(F32).astype(BF16)
    rp = lax.broadcasted_iota(jnp.int32, (C, PAIR), 0)
    cp = lax.broadcasted_iota(jnp.int32, (C, PAIR), 1) % C
    strict = cp < rp if fwd else cp > rp
    incl = cp <= rp if fwd else cp >= rp
    eye = (cp == rp).astype(F32)
    lane1 = lax.broadcasted_iota(jnp.int32, (1, PAIR), 1)
    lane2 = lax.broadcasted_iota(jnp.int32, (1, 2 * PAIR), 1) % PAIR
    lo1, hi1 = lane1 < RW_HEAD, lane1 >= RW_HEAD
    lo2, hi2 = lane2 < RW_HEAD, lane2 >= RW_HEAD
    lane_even = lax.broadcasted_iota(jnp.int32, (1, RW_WIDTH), 1) % PAIR < RW_HEAD
    r2 = lax.broadcasted_iota(jnp.int32, (PAIR, PAIR), 0)
    c2 = lax.broadcasted_iota(jnp.int32, (PAIR, PAIR), 1)
    eye2 = r2 == c2
    nt_dims = (((1,), (1,)), ((), ()))
    npair = RW_WIDTH // PAIR
    pairs = [slice(p * PAIR, (p + 1) * PAIR) for p in range(npair)]

    def prologue(c):
        rows = slice(c * C, (c + 1) * C)
        lw = lw_all[rows, :]
        L = _dot_exact_rhs_lhs(incl_bf, lw)
        Ltot = L[C - 1:C, :] if fwd else L[0:1, :]
        enL = jnp.exp(-L)
        etail = jnp.exp(Ltot - L)
        kd = kd_all[rows, :]
        bb = b_all[rows, :]
        rt = r_all[rows, :] * jnp.exp(L)
        bt = (bb * enL).astype(BF16)
        kt = (kd * enL).astype(BF16)
        zero = jnp.zeros_like(bt)
        return dict(
            c=c, rows=rows, rt=rt,
            lhs=jnp.concatenate([kk_all[rows, :] * jnp.exp(L - lw), rt], axis=0).astype(BF16),
            rhs=jnp.concatenate([jnp.where(lane_even, bt, zero), jnp.where(lane_even, zero, bt),
                                 jnp.where(lane_even, kt, zero), jnp.where(lane_even, zero, kt)], axis=0),
            ktp_t=jnp.transpose(kd * etail).astype(BF16),
            btp_t=jnp.transpose(bb * etail).astype(BF16),
            ptot=jnp.exp(Ltot),
            vv=v_all[rows, :].astype(BF16))

    for c0 in range(0, TM // C, PREP_INTERLEAVE):
        chunks = [prologue(c) for c in range(c0, c0 + PREP_INTERLEAVE)]
        items = [(ch, ps) for ch in chunks for ps in pairs]
        aa = [_dot_wkv(ch["lhs"][:, ps], ch["rhs"][:, ps], nt_dims) for ch, ps in items]
        a_b = [jnp.where(strict, m[:C, :PAIR], 0.0) for m in aa]
        a_kk = [jnp.concatenate([jnp.where(strict, m[:C, PAIR:], 0.0), jnp.where(incl, m[C:, PAIR:], 0.0)], axis=0)
                .astype(BF16) for m in aa]
        a_rb = [jnp.where(incl, m[C:, :PAIR], 0.0).astype(BF16) for m in aa]
        xs = [eye - a for a in a_b]
        pw = a_b
        for _ in range(int(np.log2(C)) - 1):
            pd = [_block_diag(p.astype(BF16), lo1, hi1) for p in pw]
            pw = [_dot_wkv(p, d) for p, d in zip(pw, pd)]
            xs = [x + _dot_wkv(x, _block_diag(p.astype(BF16), lo1, hi1)) for x, p in zip(xs, pw)]
        avv = [_dot_wkv(m, _block_diag(ch["vv"][:, ps], lo1, hi1)) for m, (ch, ps) in zip(a_kk, items)]
        wu = [_dot_wkv(t, _block_diag(jnp.concatenate([ch["lhs"][:C, ps], av[:C].astype(BF16)], axis=1), lo2, hi2))
              .astype(BF16) for t, av, (ch, ps) in zip(xs, avv, items)]
        rbwu = [_dot_wkv(m, _block_diag(x, lo2, hi2)) for m, x in zip(a_rb, wu)]
        bwu = [_dot_wkv(ch["btp_t"][ps, :], x) for x, (ch, ps) in zip(wu, items)]
        kv = [_dot_wkv(ch["ktp_t"][ps, :], ch["vv"][:, ps]) for ch, ps in items]
        for av, rb, bw, kvp, (ch, ps) in zip(avv, rbwu, bwu, kv, items):
            rows = ch["rows"]
            rq_o[z, rows, ps] = (ch["rt"][:, ps] - rb[:, :PAIR]).astype(BF16)
            y0_o[z, rows, ps] = (av[C:] - rb[:, PAIR:]).astype(BF16)
            gfull = jnp.where(eye2, ch["ptot"][:, ps], 0.0) - bw[:, :PAIR]
            hfull = kvp - bw[:, PAIR:]
            g_o[z, rows, ps] = jnp.where(lo1, gfull[:C], gfull[C:]).astype(BF16)
            h_o[z, rows, ps] = jnp.where(lo1, hfull[:C], hfull[C:]).astype(BF16)


def _dot_exact_rhs_lhs(mask_bf16, x):
    x1, x2 = _split2(x)
    return jnp.dot(mask_bf16, x1, preferred_element_type=F32) + jnp.dot(mask_bf16, x2, preferred_element_type=F32)


def _wkv_scan_kernel(gf_ref, hf_ref, rqf_ref, y0f_ref, gb_ref, hb_ref, rqb_ref, y0b_ref,
                     yf_o, yb_o, state):
    C = CHUNK
    nch = TM // C

    @pl.when(pl.program_id(1) == 0)
    def _():
        state[...] = jnp.zeros_like(state)

    pairs = [slice(p * PAIR, (p + 1) * PAIR) for p in range(RW_WIDTH // PAIR)]
    lane = lax.broadcasted_iota(jnp.int32, (1, RW_WIDTH), 1) % PAIR
    lo, hi = lane < RW_HEAD, lane >= RW_HEAD
    dirs = ((gf_ref, hf_ref, rqf_ref, y0f_ref, yf_o), (gb_ref, hb_ref, rqb_ref, y0b_ref, yb_o))
    m = [state[0], state[1]]
    for c in range(nch):
        for z, (g_ref, h_ref, rq_ref, y0_ref, y_o) in enumerate(dirs):
            cc = c if z == 0 else nch - 1 - c
            rows = slice(cc * C, (cc + 1) * C)
            g = _block_diag(g_ref[rows, :], lo, hi)
            h = _block_diag(h_ref[rows, :], lo, hi)
            rq = rq_ref[rows, :]
            mb = m[z].astype(BF16)
            ys = [jnp.dot(rq[:, ps], mb[:, ps], preferred_element_type=F32) for ps in pairs]
            ms = [jnp.dot(g[:, ps], mb[:, ps], preferred_element_type=F32) for ps in pairs]
            y_o[rows, :] = (jnp.concatenate(ys, -1) + y0_ref[rows, :]).astype(y_o.dtype)
            m[z] = jnp.concatenate(ms, -1) + h
    state[0] = m[0]
    state[1] = m[1]


def _wkv_scan(geom, g, h, rq, y0):
    W = RW_WIDTH
    nt, nct = geom.nt, geom.nct

    def fwd_tile(b, t):
        return (0, geom.tile(b, t), 0)

    def bwd_tile(b, t):
        return (1, geom.tile(b, jnp.where(t < nct, nct - 1 - t, nt - 1 - (t - nct))), 0)

    fspec = pl.BlockSpec((None, TM, W), fwd_tile)
    bspec = pl.BlockSpec((None, TM, W), bwd_tile)
    return pl.pallas_call(
        _wkv_scan_kernel,
        grid=(geom.B, nt),
        in_specs=[fspec] * 4 + [bspec] * 4,
        out_specs=[pl.BlockSpec((TM, W), lambda b, t: fwd_tile(b, t)[1:]),
                   pl.BlockSpec((TM, W), lambda b, t: bwd_tile(b, t)[1:])],
        out_shape=[jax.ShapeDtypeStruct((geom.n, W), BF16)] * 2,
        scratch_shapes=[pltpu.VMEM((2, PAIR, W), F32)],
        compiler_params=_params(("parallel", "arbitrary")),
        name="wkv_scan",
    )(g, h, rq, y0, g, h, rq, y0)


def _mix_ffn_kernel(x_ref, g1_ref, sh2_ref, sc2_ref, g2_ref,
                    yf_ref, yb_ref, gg_ref, bonus_ref, gng_ref, gnb_ref, bavg_ref,
                    attc_ref, attl_ref, conv_ref, gate_ref,
                    woa_ref, woc_ref, wor_ref, wout_ref,
                    l1g_ref, l1b_ref, w13_ref, w2_ref, l2g_ref, l2b_ref,
                    o_ref, *, alpha, nct, t0):
    y = yf_ref[...].astype(F32) + yb_ref[...].astype(F32)
    bavg = bavg_ref[...]
    mu = _dot_exact_rhs(y, bavg) * (1.0 / RW_HEAD)
    yc = y - mu
    var = _dot_exact_rhs(yc * yc, bavg) * (1.0 / RW_HEAD)
    yn = yc * lax.rsqrt(var + RW_GN_EPS) * gng_ref[...] + gnb_ref[...]
    rwo = (yn + bonus_ref[...]) * gg_ref[...]

    gl = gate_ref[...].astype(F32)
    att = jnp.where(pl.program_id(1) + t0 < nct, attc_ref[...], attl_ref[...])
    merged = (_sigmoid(gl[:, 0:D_MODEL]) * _bdot(att, woa_ref[...])
              + _sigmoid(gl[:, D_MODEL:2 * D_MODEL]) * _bdot(conv_ref[...], woc_ref[...])
              + _sigmoid(gl[:, 2 * D_MODEL:]) * _bdot(rwo, wor_ref[...]))
    o = _bdot(merged, wout_ref[...])
    x = x_ref[...]
    x1 = _layer_norm(alpha * x + g1_ref[...] * o, l1g_ref[...], l1b_ref[...])

    hmod = x1 * (1.0 + sc2_ref[...]) + sh2_ref[...]
    ug = _bdot(hmod, w13_ref[...])
    u = ug[:, :D_FF]
    f = _bdot(u * _sigmoid(u) * ug[:, D_FF:], w2_ref[...])
    o_ref[...] = _layer_norm(alpha * x1 + g2_ref[...] * f, l2g_ref[...], l2b_ref[...])


def _mix_ffn(geom, layer, alpha, x, modv, yf, yb, g, bonus, att_lat, att_ctx, conv, gate, prm, bavg, latent_only):
    W = RW_WIDTH
    nct, nlt = geom.nct, geom.nlt
    t0 = nct if latent_only else 0
    lsel = lambda b, t: (layer, 0, 0)
    vec = lambda w: pl.BlockSpec((None, 1, w), lsel)
    wspec = lambda r, c: _const_spec((None, r, c), lsel)
    cur = lambda t: t + t0
    tok = lambda w: _tok_spec(geom, w, cur)
    mod = lambda j: _mod_spec(geom, layer, j, cur)
    attc_spec = pl.BlockSpec((TM, MLA_WIDTH), lambda b, t: (b * nct + jnp.minimum(cur(t), nct - 1), 0))
    attl_spec = pl.BlockSpec((TM, MLA_WIDTH), lambda b, t: (b * nlt + jnp.maximum(cur(t) - nct, 0), 0))
    if latent_only:
        out_spec = pl.BlockSpec((TM, D_MODEL), lambda b, t: (b * nlt + t, 0))
        out_rows = geom.B * geom.seq
    else:
        out_spec = tok(D_MODEL)
        out_rows = geom.n
    return pl.pallas_call(
        functools.partial(_mix_ffn_kernel, alpha=alpha, nct=nct, t0=t0),
        grid=(geom.B, geom.nt - t0),
        in_specs=[
            tok(D_MODEL),
            mod(2), mod(3), mod(4), mod(5),
            tok(W), tok(W),
            tok(W), tok(W), vec(W), vec(W),
            pl.BlockSpec((W, W), lambda b, t: (0, 0)),
            attc_spec, attl_spec, tok(CONV_WIDTH), tok(GATE_SEG),
            wspec(MLA_WIDTH, D_MODEL), wspec(CONV_WIDTH, D_MODEL), wspec(W, D_MODEL),
            wspec(D_MODEL, D_MODEL),
            vec(D_MODEL), vec(D_MODEL),
            wspec(D_MODEL, 2 * D_FF), wspec(D_FF, D_MODEL),
            vec(D_MODEL), vec(D_MODEL),
        ],
        out_specs=out_spec,
        out_shape=jax.ShapeDtypeStruct((out_rows, D_MODEL), F32),
        compiler_params=_params(("parallel", "parallel")),
        name="mix_ffn",
    )(x, modv, modv, modv, modv, yf, yb, g, bonus, prm["rw_gn_g"], prm["rw_gn_b"], bavg,
      att_ctx, att_lat, conv, gate,
      prm["w_o_attn"], prm["w_o_conv"], prm["w_o_rwkv"], prm["w_out"],
      prm["ln1_g"], prm["ln1_b"], prm["ffn_w13"], prm["ffn_w2"], prm["ln2_g"], prm["ln2_b"])


_ROPE_SWAP = np.concatenate([np.arange(8, 16), np.arange(0, 8), np.arange(24, 32), np.arange(16, 24)])


def _rope_tables(geom):
    pos = jnp.arange(geom.seq)
    row = (pos // GRID_W).astype(F32)
    col = (pos % GRID_W).astype(F32)
    axis_dim = MLA_ROPE // 2
    inv = ROPE_BASE ** (-jnp.arange(0, axis_dim, 2, dtype=F32) / axis_dim)
    ar, ac = row[:, None] * inv, col[:, None] * inv
    cr, sr, cc, sc = jnp.cos(ar), jnp.sin(ar), jnp.cos(ac), jnp.sin(ac)
    cos32 = jnp.concatenate([cr, cr, cc, cc], -1)
    sin32 = jnp.concatenate([-sr, sr, -sc, sc], -1)
    ones = jnp.ones((geom.seq, MLA_NOPE), F32)
    zpad = jnp.zeros((geom.seq, HEAD_PAD - MLA_NOPE - MLA_ROPE), F32)
    cos_l = jnp.concatenate([ones, cos32, zpad], -1)
    sin_l = jnp.concatenate([jnp.zeros_like(ones), sin32, zpad], -1)
    cos_c = jnp.ones((geom.ctx_len, HEAD_PAD), F32)
    sin_c = jnp.zeros((geom.ctx_len, HEAD_PAD), F32)
    return jnp.concatenate([cos_c, cos_l], 0), jnp.concatenate([sin_c, sin_l], 0)


def _layout_weights(w_in, w_uq, w_ukv):
    L = w_in.shape[0]
    o_q, o_kv, o_kr = MLA_Q_LORA, MLA_Q_LORA + MLA_KV_LORA, MLA_Q_LORA + MLA_KV_LORA + MLA_ROPE
    krope = w_in[:, :, o_kv:o_kr]
    zl = jnp.zeros((L, D_MODEL, MLA_NOPE), F32)
    zr = jnp.zeros((L, D_MODEL, HEAD_PAD - MLA_NOPE - MLA_ROPE), F32)
    w_in_p = jnp.concatenate(
        [w_in[:, :, :o_kv], zl, krope, zr, zl, krope[:, :, _ROPE_SWAP], zr, w_in[:, :, o_kr:]], -1).astype(BF16)

    wq = w_uq.reshape(L, MLA_Q_LORA, MLA_HEADS, MLA_NOPE + MLA_ROPE)
    q_rope = wq[..., MLA_NOPE:]
    zq = jnp.zeros((L, MLA_Q_LORA, MLA_HEADS, HEAD_PAD - MLA_NOPE - MLA_ROPE), F32)
    zn = jnp.zeros((L, MLA_Q_LORA, MLA_HEADS, MLA_NOPE), F32)
    wq_p = jnp.concatenate([wq, zq], -1).reshape(L, MLA_Q_LORA, -1).astype(BF16)
    wqs_p = jnp.concatenate([zn, q_rope[..., _ROPE_SWAP], zq], -1).reshape(L, MLA_Q_LORA, -1).astype(BF16)

    wkv = w_ukv.reshape(L, MLA_KV_LORA, MLA_HEADS, MLA_NOPE + MLA_V)
    zk = jnp.zeros((L, MLA_KV_LORA, MLA_HEADS, HEAD_PAD - MLA_NOPE), F32)
    wk_p = jnp.concatenate([wkv[..., :MLA_NOPE], zk], -1).reshape(L, MLA_KV_LORA, -1).astype(BF16)
    zv = jnp.zeros((L, MLA_KV_LORA, MLA_HEADS, HEAD_PAD - MLA_V), F32)
    wv_p = jnp.concatenate([wkv[..., MLA_NOPE:], zv], -1).reshape(L, MLA_KV_LORA, -1).astype(BF16)
    return w_in_p, wq_p, wqs_p, wk_p, wv_p


def _head_block_ones():
    idx = np.arange(RW_WIDTH) // RW_HEAD
    return jnp.asarray((idx[:, None] == idx[None, :]).astype(np.float32), dtype=BF16)


def kernel(x, c, ctx, c_ctx, mod_w, mod_b, w_in, q_norm, w_uq, kv_norm, w_ukv, w_o_attn, conv_w, w_o_conv,
           rw_mu, rw_w0, rw_w_up, rw_a0, rw_a_up, rw_g_up, rw_k_k, rw_k_a, rw_r_k, rw_gn_g, rw_gn_b,
           w_o_rwkv, w_out, ln1_g, ln1_b, ffn_w13, ffn_w2, ln2_g, ln2_b):
    B, seq, _ = x.shape
    ctx_len = ctx.shape[1]
    L = mod_w.shape[0]
    geom = _Geom(B, ctx_len, seq)
    alpha = (2.0 * L) ** 0.25

    rows = -(-(B + 1) // HALO) * HALO
    c_all = jnp.concatenate([c, c_ctx[None, :], jnp.zeros((rows - B - 1, D_MODEL), F32)], 0)
    modv = _mod_vectors(c_all, mod_w, mod_b).reshape(L, rows, 1, -1)

    w_in_p, wq_p, wqs_p, wk_p, wv_p = _layout_weights(w_in, w_uq, w_ukv)
    cos_t, sin_t = _rope_tables(geom)
    bsum = _head_block_ones()
    vec3 = lambda a: a.reshape(L, 1, -1)
    prm = dict(
        rw_mu=vec3(rw_mu), rw_w0=rw_w0, rw_w_up=rw_w_up, rw_a0=rw_a0, rw_a_up=rw_a_up, rw_g_up=rw_g_up,
        rw_k_k=vec3(rw_k_k), rw_k_a=vec3(rw_k_a), rw_r_k=vec3(rw_r_k), conv_w=conv_w,
        rw_gn_g=vec3(rw_gn_g), rw_gn_b=vec3(rw_gn_b),
        w_o_attn=w_o_attn.astype(BF16), w_o_conv=w_o_conv.astype(BF16), w_o_rwkv=w_o_rwkv.astype(BF16),
        w_out=w_out.astype(BF16), ln1_g=vec3(ln1_g), ln1_b=vec3(ln1_b),
        ffn_w13=ffn_w13.astype(BF16), ffn_w2=ffn_w2.astype(BF16), ln2_g=vec3(ln2_g), ln2_b=vec3(ln2_b),
    )
    qn, kvn = vec3(q_norm), vec3(kv_norm)

    xs = jnp.concatenate([ctx.reshape(geom.nc, D_MODEL), x.reshape(B * seq, D_MODEL)], axis=0)
    for l in range(L):
        mla, conv, rw, gate = _in_proj(geom, l, xs, modv, w_in_p)
        q, k, v = _mla_proj(geom, l, mla, qn, kvn, wq_p, wqs_p, wk_p, wv_p, cos_t, sin_t)
        att_lat, att_ctx = _attention(geom, q, k, v)
        g, bonus, cv, gm, hm, rq, y0 = _features(geom, l, rw, conv, prm, bsum)
        yf, yb = _wkv_scan(geom, gm, hm, rq, y0)
        xs = _mix_ffn(geom, l, alpha, xs, modv, yf, yb, g, bonus, att_lat, att_ctx, cv, gate, prm, bsum,
                      latent_only=(l == L - 1))
    return xs.reshape(B, seq, D_MODEL)
```

```python
import functools

import numpy as np
import jax
import jax.numpy as jnp
from jax import lax
from jax.experimental import pallas as pl
from jax.experimental.pallas import tpu as pltpu

F32 = jnp.float32
BF16 = jnp.bfloat16

D_MODEL = 1024
GRID_W = 64
MLA_HEADS = 8
MLA_Q_LORA = 384
MLA_KV_LORA = 256
MLA_NOPE = 64
MLA_ROPE = 32
MLA_V = 64
MLA_WIDTH = MLA_HEADS * MLA_V
ROPE_BASE = 10000.0
CONV_WIDTH = 512
RW_HEADS = 8
RW_HEAD = 64
RW_WIDTH = RW_HEADS * RW_HEAD
RW_DECAY_LORA = 64
RW_ICLR_LORA = 64
RW_GATE_LORA = 128
RW_GN_EPS = 64e-5
RW_IN = 3 * RW_WIDTH + 2 * RW_DECAY_LORA + 2 * RW_ICLR_LORA + RW_GATE_LORA
N_BRANCH = 3
D_FF = 2816
LN_EPS = 1e-5
RMS_EPS = 1e-6

LANES = 128
HEAD_PAD = LANES
TM = 256
TM_BIG = 512
CHUNK = 64
PAIR = 2 * RW_HEAD
assert PAIR == LANES
PREP_INTERLEAVE = 4
HALO = 16
VMEM_LIMIT = 56 * 1024 * 1024

MLA_SEG = MLA_Q_LORA + MLA_KV_LORA + 2 * HEAD_PAD
CONV_SEG = 3 * CONV_WIDTH
GATE_SEG = N_BRANCH * D_MODEL
IN_SEGS = (MLA_SEG, CONV_SEG, RW_IN, GATE_SEG)
IN_OFFS = tuple(int(v) for v in np.cumsum((0,) + IN_SEGS))


def _bdot(a, b):
    return jnp.dot(a.astype(BF16), b.astype(BF16), preferred_element_type=F32)


def _split2(a):
    hi = a.astype(BF16)
    lo = (a - hi.astype(F32)).astype(BF16)
    return hi, lo


def _dot3(a, b, dims=None):
    ah, al = _split2(a)
    bh, bl = _split2(b)
    if dims is None:
        f = lambda u, v: jnp.dot(u, v, preferred_element_type=F32)
    else:
        f = lambda u, v: lax.dot_general(u, v, dims, preferred_element_type=F32)
    return f(ah, bh) + (f(ah, bl) + f(al, bh))


def _dot1(a, b, dims=None):
    a, b = a.astype(BF16), b.astype(BF16)
    if dims is None:
        return jnp.dot(a, b, preferred_element_type=F32)
    return lax.dot_general(a, b, dims, preferred_element_type=F32)


_dot_wkv = _dot1


def _dot_exact_rhs(a, b_bf16):
    a1, a2 = _split2(a)
    return jnp.dot(a1, b_bf16, preferred_element_type=F32) + jnp.dot(a2, b_bf16, preferred_element_type=F32)


def _sigmoid(x):
    return 0.5 * jnp.tanh(0.5 * x) + 0.5


def _layer_norm(x, g, b):
    mu = jnp.mean(x, -1, keepdims=True)
    xc = x - mu
    var = jnp.mean(xc * xc, -1, keepdims=True)
    return xc * lax.rsqrt(var + LN_EPS) * g + b


def _const_spec(shape, index_map):
    return pl.BlockSpec(shape, index_map, pipeline_mode=pl.Buffered(1))


def _params(sem):
    return pltpu.CompilerParams(dimension_semantics=sem, vmem_limit_bytes=VMEM_LIMIT)


def _mod_kernel(c_ref, w_ref, b_ref, o_ref):
    c = c_ref[...]
    s = c * _sigmoid(c)
    o_ref[...] = _dot3(s, w_ref[...]) + b_ref[...]


def _mod_vectors(c_all, mod_w, mod_b):
    L = mod_w.shape[0]
    R = c_all.shape[0]
    n = mod_w.shape[2] // D_MODEL
    return pl.pallas_call(
        _mod_kernel,
        grid=(L, n),
        in_specs=[
            pl.BlockSpec((R, D_MODEL), lambda l, j: (0, 0)),
            pl.BlockSpec((None, D_MODEL, D_MODEL), lambda l, j: (l, 0, j)),
            pl.BlockSpec((None, 1, D_MODEL), lambda l, j: (l, 0, j)),
        ],
        out_specs=pl.BlockSpec((None, R, D_MODEL), lambda l, j: (l, 0, j)),
        out_shape=jax.ShapeDtypeStruct((L, R, n * D_MODEL), F32),
        compiler_params=_params(("parallel", "parallel")),
        name="mod_vectors",
    )(c_all, mod_w, mod_b.reshape(L, 1, -1))


class _Geom:
    def __init__(self, B, ctx_len, seq):
        assert ctx_len % TM == 0 and seq % TM_BIG == 0 and seq % GRID_W == 0
        self.B, self.ctx_len, self.seq = B, ctx_len, seq
        self.tt = ctx_len + seq
        self.nct = ctx_len // TM
        self.nlt = seq // TM
        self.nt = self.nct + self.nlt
        self.nc = B * ctx_len
        self.n = B * self.tt
        assert self.nc % TM_BIG == 0 and self.nc % seq == 0

    def tile(self, b, t):
        return jnp.where(t < self.nct, b * self.nct + t, self.B * self.nct + b * self.nlt + (t - self.nct))

    def mod_row(self, b, t):
        return jnp.where(t < self.nct, self.B, b)


def _mod_spec(geom, layer, j, tmap=lambda t: t):
    return pl.BlockSpec((None, None, 1, D_MODEL), lambda b, t: (layer, geom.mod_row(b, tmap(t)), 0, j))


def _tok_spec(geom, width, tmap=lambda t: t):
    return pl.BlockSpec((TM, width), lambda b, t: (geom.tile(b, tmap(t)), 0))


def _in_proj_kernel(x_ref, sh_ref, sc_ref, w_ref, mla_ref, conv_ref, rw_ref, gate_ref):
    h = (x_ref[...] * (1.0 + sc_ref[...]) + sh_ref[...]).astype(BF16)
    for o_ref, lo, hi in zip((mla_ref, conv_ref, rw_ref, gate_ref), IN_OFFS[:-1], IN_OFFS[1:]):
        o_ref[...] = jnp.dot(h, w_ref[:, lo:hi], preferred_element_type=F32).astype(BF16)


def _in_proj(geom, layer, x, modv, w_in_p):
    ctx_tiles = geom.nc // TM_BIG
    per_row = geom.seq // TM_BIG

    def mod(j):
        row = lambda i: jnp.where(i < ctx_tiles, geom.B, (i - ctx_tiles) // per_row)
        return pl.BlockSpec((None, None, 1, D_MODEL), lambda i: (layer, row(i), 0, j))

    big = lambda w: pl.BlockSpec((TM_BIG, w), lambda i: (i, 0))
    return pl.pallas_call(
        _in_proj_kernel,
        grid=(geom.n // TM_BIG,),
        in_specs=[big(D_MODEL), mod(0), mod(1),
                  _const_spec((None, D_MODEL, IN_OFFS[-1]), lambda i: (layer, 0, 0))],
        out_specs=[big(w) for w in IN_SEGS],
        out_shape=[jax.ShapeDtypeStruct((geom.n, w), BF16) for w in IN_SEGS],
        compiler_params=_params(("parallel",)),
        name="in_proj",
    )(x, modv, modv, w_in_p)


def _mla_kernel(m_ref, qn_ref, kvn_ref, wq_ref, wqs_ref, wk_ref, wv_ref, cos_ref, sin_ref,
                q_ref, k_ref, v_ref):
    m = m_ref[...].astype(F32)
    cq = m[:, :MLA_Q_LORA]
    ckv = m[:, MLA_Q_LORA:MLA_Q_LORA + MLA_KV_LORA]
    kslab = m[:, MLA_Q_LORA + MLA_KV_LORA:MLA_Q_LORA + MLA_KV_LORA + HEAD_PAD]
    kslab_sw = m[:, MLA_Q_LORA + MLA_KV_LORA + HEAD_PAD:]
    cqn = (cq * lax.rsqrt(jnp.mean(cq * cq, -1, keepdims=True) + RMS_EPS) * qn_ref[...]).astype(BF16)
    ckvn = (ckv * lax.rsqrt(jnp.mean(ckv * ckv, -1, keepdims=True) + RMS_EPS) * kvn_ref[...]).astype(BF16)
    cos_t = cos_ref[...]
    sin_t = sin_ref[...]
    qa = jnp.dot(cqn, wq_ref[...], preferred_element_type=F32)
    qb = jnp.dot(cqn, wqs_ref[...], preferred_element_type=F32)
    kn = jnp.dot(ckvn, wk_ref[...], preferred_element_type=F32)
    kr = kslab * cos_t + kslab_sw * sin_t
    scale = (MLA_NOPE + MLA_ROPE) ** -0.5
    for h in range(MLA_HEADS):
        sl = slice(h * HEAD_PAD, (h + 1) * HEAD_PAD)
        q_ref[:, sl] = ((qa[:, sl] * cos_t + qb[:, sl] * sin_t) * scale).astype(BF16)
        k_ref[:, sl] = (kn[:, sl] + kr).astype(BF16)
    lane = lax.broadcasted_iota(jnp.int32, (1, MLA_HEADS * HEAD_PAD), 1) % HEAD_PAD
    v_ref[...] = (jnp.dot(ckvn, wv_ref[...], preferred_element_type=F32)
                  + (lane == MLA_V).astype(F32)).astype(BF16)


def _mla_proj(geom, layer, mla, q_norm, kv_norm, wq, wqs, wk, wv, cos_t, sin_t):
    hw = MLA_HEADS * HEAD_PAD
    lsel = lambda b, t: (layer, 0, 0)
    return pl.pallas_call(
        _mla_kernel,
        grid=(geom.B, geom.nt),
        in_specs=[
            _tok_spec(geom, MLA_SEG),
            pl.BlockSpec((None, 1, MLA_Q_LORA), lsel),
            pl.BlockSpec((None, 1, MLA_KV_LORA), lsel),
            _const_spec((None, MLA_Q_LORA, hw), lsel),
            _const_spec((None, MLA_Q_LORA, hw), lsel),
            _const_spec((None, MLA_KV_LORA, hw), lsel),
            _const_spec((None, MLA_KV_LORA, hw), lsel),
            pl.BlockSpec((TM, HEAD_PAD), lambda b, t: (t, 0)),
            pl.BlockSpec((TM, HEAD_PAD), lambda b, t: (t, 0)),
        ],
        out_specs=[_tok_spec(geom, hw)] * 3,
        out_shape=[jax.ShapeDtypeStruct((geom.n, hw), BF16)] * 3,
        compiler_params=_params(("parallel", "parallel")),
        name="mla_proj",
    )(mla, q_norm, kv_norm, wq, wqs, wk, wv, cos_t, sin_t)


HEADS_PER_STEP = 4


def _attend(q_ref, kv_refs, o_ref):
    heads = [slice(j * HEAD_PAD, (j + 1) * HEAD_PAD) for j in range(HEADS_PER_STEP)]

    def scores(hs):
        q = q_ref[:, hs]
        return [lax.dot_general(q, k_ref[:, hs], (((1,), (1,)), ((), ())), preferred_element_type=F32)
                for k_ref, _ in kv_refs]

    def probs(ss):
        m = functools.reduce(jnp.maximum, [jnp.max(s, -1, keepdims=True) for s in ss])
        return [jnp.exp(s - m).astype(BF16) for s in ss]

    def weighted(ps, hs):
        pv = sum(jnp.dot(p, v_ref[:, hs], preferred_element_type=F32) for p, (_, v_ref) in zip(ps, kv_refs))
        return pv[:, :MLA_V] / pv[:, MLA_V:MLA_V + 1]

    n = len(heads)
    ss = [scores(heads[0]), scores(heads[1])] + [None] * (n - 2)
    ps = [probs(ss[0])] + [None] * (n - 1)
    outs = []
    for j in range(n):
        if j + 2 < n:
            ss[j + 2] = scores(heads[j + 2])
        if j + 1 < n:
            ps[j + 1] = probs(ss[j + 1])
        outs.append(weighted(ps[j], heads[j]))
    o_ref[...] = jnp.concatenate(outs, -1).astype(BF16)


def _attn_kernel(ql_ref, qc_ref, kc_ref, vc_ref, kl_ref, vl_ref, ol_ref, oc_ref):
    _attend(ql_ref, [(kc_ref, vc_ref), (kl_ref, vl_ref)], ol_ref)

    @pl.when(pl.program_id(2) == 0)
    def _():
        _attend(qc_ref, [(kc_ref, vc_ref)], oc_ref)


def _attention(geom, q, k, v):
    qw = HEADS_PER_STEP * HEAD_PAD
    vw = HEADS_PER_STEP * MLA_V
    per_row = geom.seq // TM_BIG
    lat0 = geom.nc // TM_BIG
    seq0 = geom.nc // geom.seq
    ctx_spec = pl.BlockSpec((geom.ctx_len, qw), lambda b, h, t: (b, h))
    lat_spec = pl.BlockSpec((geom.seq, qw), lambda b, h, t: (seq0 + b, h))
    return pl.pallas_call(
        _attn_kernel,
        grid=(geom.B, MLA_HEADS // HEADS_PER_STEP, per_row),
        in_specs=[
            pl.BlockSpec((TM_BIG, qw), lambda b, h, t: (lat0 + b * per_row + t, h)),
            ctx_spec, ctx_spec, ctx_spec, lat_spec, lat_spec,
        ],
        out_specs=[pl.BlockSpec((TM_BIG, vw), lambda b, h, t: (b * per_row + t, h)),
                   pl.BlockSpec((geom.ctx_len, vw), lambda b, h, t: (b, h))],
        out_shape=[jax.ShapeDtypeStruct((geom.B * geom.seq, MLA_WIDTH), BF16),
                   jax.ShapeDtypeStruct((geom.nc, MLA_WIDTH), BF16)],
        compiler_params=_params(("parallel", "parallel", "arbitrary")),
        name="attn",
    )(q, q, k, v, k, v)


def _shift_prev(x, halo_row):
    rows = lax.broadcasted_iota(jnp.int32, x.shape, 0)
    return jnp.where(rows == 0, halo_row, pltpu.roll(x, 1, 0))


def _shift_next(x, halo_row):
    rows = lax.broadcasted_iota(jnp.int32, x.shape, 0)
    return jnp.where(rows == x.shape[0] - 1, halo_row, pltpu.roll(x, x.shape[0] - 1, 0))


def _feat_kernel(rw_ref, rwp_ref, rwn_ref, cv_ref, cvp_ref, cvn_ref,
                 mu_ref, w0_ref, wup_ref, a0_ref, aup_ref, gup_ref, kk_ref, ka_ref, rk_ref,
                 cw_ref, bsum_ref,
                 g_o, bonus_o, conv_o, gm_o, hm_o, rq_o, y0_o, *, nct, nt):
    t = pl.program_id(1)
    first = jnp.logical_or(t == 0, t == nct).astype(F32)
    last = jnp.logical_or(t == nct - 1, t == nt - 1).astype(F32)
    keep_prev = 1.0 - first
    keep_next = 1.0 - last
    W = RW_WIDTH

    def gated(ref):
        blk = ref[...].astype(F32)
        return blk[..., 2 * CONV_WIDTH:] * blk[..., :CONV_WIDTH]
    cblk = cv_ref[...].astype(F32)
    up = cblk[:, 2 * CONV_WIDTH:] * cblk[:, :CONV_WIDTH]
    up_p = _shift_prev(up, gated(cvp_ref)[HALO - 1:HALO] * keep_prev)
    up_n = _shift_next(up, gated(cvn_ref)[0:1] * keep_next)
    cw = cw_ref[...]
    conv_o[...] = (cblk[:, CONV_WIDTH:2 * CONV_WIDTH]
                   * (up_p * cw[0:1] + up * cw[1:2] + up_n * cw[2:3])).astype(BF16)

    p = rw_ref[...].astype(F32)
    p_prev = _shift_prev(p, rwp_ref[HALO - 1:HALO, :].astype(F32) * keep_prev)
    p_next = _shift_next(p, rwn_ref[0:1, :].astype(F32) * keep_next)
    p = p + (0.5 * (p_prev + p_next) - p) * mu_ref[...]
    r = p[:, 0:W]
    k = p[:, W:2 * W]
    v = p[:, 2 * W:3 * W]
    wd = p[:, 3 * W:3 * W + 2 * RW_DECAY_LORA]
    ad = p[:, 3 * W + 2 * RW_DECAY_LORA:3 * W + 2 * RW_DECAY_LORA + 2 * RW_ICLR_LORA]
    gd = p[:, 3 * W + 2 * RW_DECAY_LORA + 2 * RW_ICLR_LORA:]

    bsum = bsum_ref[...]
    kkf = k * kk_ref[...]
    kk = kkf * lax.rsqrt(jnp.maximum(_dot_exact_rhs(kkf * kkf, bsum), 1e-24))
    g_o[...] = _dot3(_sigmoid(gd), gup_ref[...]).astype(BF16)
    bonus_o[...] = (_dot_exact_rhs(r * k * rk_ref[...], bsum) * v).astype(BF16)
    for z in range(2):
        wdz = jnp.tanh(wd[:, z * RW_DECAY_LORA:(z + 1) * RW_DECAY_LORA])
        adz = ad[:, z * RW_ICLR_LORA:(z + 1) * RW_ICLR_LORA]
        tz = w0_ref[z:z + 1, :] + _dot3(wdz, wup_ref[z])
        lw = _sigmoid(tz) * (-float(np.exp(-0.5)))
        a = _sigmoid(a0_ref[z:z + 1, :] + _dot3(adz, aup_ref[z]))
        kd = k * (1.0 + (a - 1.0) * ka_ref[...])
        _wkv_chunk_reduce_pairs(z, r, v, kk, kd, kk * a, lw, gm_o, hm_o, rq_o, y0_o)


def _features(geom, layer, rw, conv, prm, bsum):
    nt, n = geom.nt, geom.n
    blocks_per_tile = TM // HALO
    last_halo_block = n // HALO - 1

    def prev_spec(w):
        return pl.BlockSpec((HALO, w), lambda b, t: (jnp.maximum(geom.tile(b, t) * blocks_per_tile - 1, 0), 0))

    def next_spec(w):
        return pl.BlockSpec(
            (HALO, w), lambda b, t: (jnp.minimum((geom.tile(b, t) + 1) * blocks_per_tile, last_halo_block), 0))

    lsel2 = lambda b, t: (layer, 0, 0)
    lsel3 = lambda b, t: (layer, 0, 0, 0)
    W = RW_WIDTH
    tok = _tok_spec(geom, W)
    dir_spec = pl.BlockSpec((2, TM, W), lambda b, t: (0, geom.tile(b, t), 0))
    tok_shape = jax.ShapeDtypeStruct((n, W), BF16)
    dir_shape = jax.ShapeDtypeStruct((2, n, W), BF16)
    return pl.pallas_call(
        functools.partial(_feat_kernel, nct=geom.nct, nt=nt),
        grid=(geom.B, nt),
        in_specs=[
            _tok_spec(geom, RW_IN), prev_spec(RW_IN), next_spec(RW_IN),
            _tok_spec(geom, CONV_SEG), prev_spec(CONV_SEG), next_spec(CONV_SEG),
            pl.BlockSpec((None, 1, RW_IN), lsel2),
            pl.BlockSpec((None, 2, W), lsel2),
            pl.BlockSpec((None, 2, RW_DECAY_LORA, W), lsel3),
            pl.BlockSpec((None, 2, W), lsel2),
            pl.BlockSpec((None, 2, RW_ICLR_LORA, W), lsel3),
            pl.BlockSpec((None, RW_GATE_LORA, W), lsel2),
            pl.BlockSpec((None, 1, W), lsel2),
            pl.BlockSpec((None, 1, W), lsel2),
            pl.BlockSpec((None, 1, W), lsel2),
            pl.BlockSpec((None, 3, CONV_WIDTH), lsel2),
            pl.BlockSpec((W, W), lambda b, t: (0, 0)),
        ],
        out_specs=[tok] * 3 + [dir_spec] * 4,
        out_shape=[tok_shape] * 3 + [dir_shape] * 4,
        compiler_params=_params(("parallel", "parallel")),
        name="feat_wkv_prep",
    )(rw, rw, rw, conv, conv, conv,
      prm["rw_mu"], prm["rw_w0"], prm["rw_w_up"], prm["rw_a0"], prm["rw_a_up"], prm["rw_g_up"],
      prm["rw_k_k"], prm["rw_k_a"], prm["rw_r_k"], prm["conv_w"], bsum)


def _block_diag(x, lo, hi):
    zero = jnp.zeros_like(x)
    return jnp.concatenate([jnp.where(lo, x, zero), jnp.where(hi, x, zero)], axis=0)


def _wkv_chunk_reduce_pairs(z, r_all, v_all, kk_all, kd_all, b_all, lw_all, g_o, h_o, rq_o, y0_o):
    fwd = z == 0
    C = CHUNK
    assert C == RW_HEAD
    ri = lax.broadcasted_iota(jnp.int32, (C, C), 0)
    ci = lax.broadcasted_iota(jnp.int32, (C, C), 1)
    incl_bf = ((ci <= ri) if fwd else (ci >= ri)).astype(F32).astype(BF16)
    rp = lax.broadcasted_iota(jnp.int32, (C, PAIR), 0)
    cp = lax.broadcasted_iota(jnp.int32, (C, PAIR), 1) % C
    strict = cp < rp if fwd else cp > rp
    incl = cp <= rp if fwd else cp >= rp
    eye = (cp == rp).astype(F32)
    lane1 = lax.broadcasted_iota(jnp.int32, (1, PAIR), 1)
    lane2 = lax.broadcasted_iota(jnp.int32, (1, 2 * PAIR), 1) % PAIR
    lo1, hi1 = lane1 < RW_HEAD, lane1 >= RW_HEAD
    lo2, hi2 = lane2 < RW_HEAD, lane2 >= RW_HEAD
    lane_even = lax.broadcasted_iota(jnp.int32, (1, RW_WIDTH), 1) % PAIR < RW_HEAD
    r2 = lax.broadcasted_iota(jnp.int32, (PAIR, PAIR), 0)
    c2 = lax.broadcasted_iota(jnp.int32, (PAIR, PAIR), 1)
    eye2 = r2 == c2
    nt_dims = (((1,), (1,)), ((), ()))
    npair = RW_WIDTH // PAIR
    pairs = [slice(p * PAIR, (p + 1) * PAIR) for p in range(npair)]

    def prologue(c):
        rows = slice(c * C, (c + 1) * C)
        lw = lw_all[rows, :]
        L = _dot_exact_rhs_lhs(incl_bf, lw)
        Ltot = L[C - 1:C, :] if fwd else L[0:1, :]
        enL = jnp.exp(-L)
        etail = jnp.exp(Ltot - L)
        kd = kd_all[rows, :]
        bb = b_all[rows, :]
        rt = r_all[rows, :] * jnp.exp(L)
        bt = (bb * enL).astype(BF16)
        kt = (kd * enL).astype(BF16)
        zero = jnp.zeros_like(bt)
        return dict(
            c=c, rows=rows, rt=rt,
            lhs=jnp.concatenate([kk_all[rows, :] * jnp.exp(L - lw), rt], axis=0).astype(BF16),
            rhs=jnp.concatenate([jnp.where(lane_even, bt, zero), jnp.where(lane_even, zero, bt),
                                 jnp.where(lane_even, kt, zero), jnp.where(lane_even, zero, kt)], axis=0),
            ktp_t=jnp.transpose(kd * etail).astype(BF16),
            btp_t=jnp.transpose(bb * etail).astype(BF16),
            ptot=jnp.exp(Ltot),
            vv=v_all[rows, :].astype(BF16))

    for c0 in range(0, TM // C, PREP_INTERLEAVE):
        chunks = [prologue(c) for c in range(c0, c0 + PREP_INTERLEAVE)]
        items = [(ch, ps) for ch in chunks for ps in pairs]
        aa = [_dot_wkv(ch["lhs"][:, ps], ch["rhs"][:, ps], nt_dims) for ch, ps in items]
        a_b = [jnp.where(strict, m[:C, :PAIR], 0.0) for m in aa]
        a_kk = [jnp.concatenate([jnp.where(strict, m[:C, PAIR:], 0.0), jnp.where(incl, m[C:, PAIR:], 0.0)], axis=0)
                .astype(BF16) for m in aa]
        a_rb = [jnp.where(incl, m[C:, :PAIR], 0.0).astype(BF16) for m in aa]
        bd = lambda p: _block_diag(p.astype(BF16), lo1, hi1)
        pw = [_dot_wkv(a, bd(a)) for a in a_b]
        xs = [eye - a for a in a_b]
        rounds = int(np.log2(C)) - 1
        for i in range(rounds):
            if i + 1 < rounds:
                both = [_dot_wkv(jnp.concatenate([p, x], axis=0), bd(p)) for p, x in zip(pw, xs)]
                pw = [m[:C] for m in both]
                xs = [x + m[C:] for x, m in zip(xs, both)]
            else:
                xs = [x + _dot_wkv(x, bd(p)) for x, p in zip(xs, pw)]
        avv = [_dot_wkv(m, _block_diag(ch["vv"][:, ps], lo1, hi1)) for m, (ch, ps) in zip(a_kk, items)]
        wu = [_dot_wkv(t, _block_diag(jnp.concatenate([ch["lhs"][:C, ps], av[:C].astype(BF16)], axis=1), lo2, hi2))
              .astype(BF16) for t, av, (ch, ps) in zip(xs, avv, items)]
        rbwu = [_dot_wkv(m, _block_diag(x, lo2, hi2)) for m, x in zip(a_rb, wu)]
        bwu = [_dot_wkv(ch["btp_t"][ps, :], x) for x, (ch, ps) in zip(wu, items)]
        kv = [_dot_wkv(ch["ktp_t"][ps, :], ch["vv"][:, ps]) for ch, ps in items]
        for av, rb, bw, kvp, (ch, ps) in zip(avv, rbwu, bwu, kv, items):
            rows = ch["rows"]
            rq_o[z, rows, ps] = (ch["rt"][:, ps] - rb[:, :PAIR]).astype(BF16)
            y0_o[z, rows, ps] = (av[C:] - rb[:, PAIR:]).astype(BF16)
            gfull = jnp.where(eye2, ch["ptot"][:, ps], 0.0) - bw[:, :PAIR]
            hfull = kvp - bw[:, PAIR:]
            g_o[z, rows, ps] = jnp.where(lo1, gfull[:C], gfull[C:]).astype(BF16)
            h_o[z, rows, ps] = jnp.where(lo1, hfull[:C], hfull[C:]).astype(BF16)


def _dot_exact_rhs_lhs(mask_bf16, x):
    x1, x2 = _split2(x)
    return jnp.dot(mask_bf16, x1, preferred_element_type=F32) + jnp.dot(mask_bf16, x2, preferred_element_type=F32)


def _wkv_scan_kernel(gf_ref, hf_ref, rqf_ref, y0f_ref, gb_ref, hb_ref, rqb_ref, y0b_ref,
                     yf_o, yb_o, state):
    C = CHUNK
    nch = TM // C

    @pl.when(pl.program_id(1) == 0)
    def _():
        state[...] = jnp.zeros_like(state)

    pairs = [slice(p * PAIR, (p + 1) * PAIR) for p in range(RW_WIDTH // PAIR)]
    lane = lax.broadcasted_iota(jnp.int32, (1, RW_WIDTH), 1) % PAIR
    lo, hi = lane < RW_HEAD, lane >= RW_HEAD
    dirs = ((gf_ref, hf_ref, rqf_ref, y0f_ref, yf_o), (gb_ref, hb_ref, rqb_ref, y0b_ref, yb_o))
    m = [state[0], state[1]]
    for c in range(nch):
        for z, (g_ref, h_ref, rq_ref, y0_ref, y_o) in enumerate(dirs):
            cc = c if z == 0 else nch - 1 - c
            rows = slice(cc * C, (cc + 1) * C)
            g = _block_diag(g_ref[rows, :], lo, hi)
            h = _block_diag(h_ref[rows, :], lo, hi)
            rq = rq_ref[rows, :]
            mb = m[z].astype(BF16)
            ys = [jnp.dot(rq[:, ps], mb[:, ps], preferred_element_type=F32) for ps in pairs]
            ms = [jnp.dot(g[:, ps], mb[:, ps], preferred_element_type=F32) for ps in pairs]
            y_o[rows, :] = (jnp.concatenate(ys, -1) + y0_ref[rows, :]).astype(y_o.dtype)
            m[z] = jnp.concatenate(ms, -1) + h
    state[0] = m[0]
    state[1] = m[1]


def _wkv_scan(geom, g, h, rq, y0):
    W = RW_WIDTH
    nt, nct = geom.nt, geom.nct

    def fwd_tile(b, t):
        return (0, geom.tile(b, t), 0)

    def bwd_tile(b, t):
        return (1, geom.tile(b, jnp.where(t < nct, nct - 1 - t, nt - 1 - (t - nct))), 0)

    fspec = pl.BlockSpec((None, TM, W), fwd_tile)
    bspec = pl.BlockSpec((None, TM, W), bwd_tile)
    return pl.pallas_call(
        _wkv_scan_kernel,
        grid=(geom.B, nt),
        in_specs=[fspec] * 4 + [bspec] * 4,
        out_specs=[pl.BlockSpec((TM, W), lambda b, t: fwd_tile(b, t)[1:]),
                   pl.BlockSpec((TM, W), lambda b, t: bwd_tile(b, t)[1:])],
        out_shape=[jax.ShapeDtypeStruct((geom.n, W), BF16)] * 2,
        scratch_shapes=[pltpu.VMEM((2, PAIR, W), F32)],
        compiler_params=_params(("parallel", "arbitrary")),
        name="wkv_scan",
    )(g, h, rq, y0, g, h, rq, y0)


def _mix_ffn_kernel(x_ref, g1_ref, sh2_ref, sc2_ref, g2_ref,
                    yf_ref, yb_ref, gg_ref, bonus_ref, gng_ref, gnb_ref, bavg_ref,
                    attc_ref, attl_ref, conv_ref, gate_ref,
                    woa_ref, woc_ref, wor_ref, wout_ref,
                    l1g_ref, l1b_ref, w13_ref, w2_ref, l2g_ref, l2b_ref,
                    o_ref, *, alpha, nct, t0):
    y = yf_ref[...].astype(F32) + yb_ref[...].astype(F32)
    bavg = bavg_ref[...]
    mu = _dot_exact_rhs(y, bavg) * (1.0 / RW_HEAD)
    yc = y - mu
    var = _dot_exact_rhs(yc * yc, bavg) * (1.0 / RW_HEAD)
    yn = yc * lax.rsqrt(var + RW_GN_EPS) * gng_ref[...] + gnb_ref[...]
    rwo = (yn + bonus_ref[...]) * gg_ref[...]

    gl = gate_ref[...].astype(F32)
    att = jnp.where(pl.program_id(1) + t0 < nct, attc_ref[...], attl_ref[...])
    merged = (_sigmoid(gl[:, 0:D_MODEL]) * _bdot(att, woa_ref[...])
              + _sigmoid(gl[:, D_MODEL:2 * D_MODEL]) * _bdot(conv_ref[...], woc_ref[...])
              + _sigmoid(gl[:, 2 * D_MODEL:]) * _bdot(rwo, wor_ref[...]))
    o = _bdot(merged, wout_ref[...])
    x = x_ref[...]
    x1 = _layer_norm(alpha * x + g1_ref[...] * o, l1g_ref[...], l1b_ref[...])

    hmod = x1 * (1.0 + sc2_ref[...]) + sh2_ref[...]
    ug = _bdot(hmod, w13_ref[...])
    u = ug[:, :D_FF]
    f = _bdot(u * _sigmoid(u) * ug[:, D_FF:], w2_ref[...])
    o_ref[...] = _layer_norm(alpha * x1 + g2_ref[...] * f, l2g_ref[...], l2b_ref[...])


def _mix_ffn(geom, layer, alpha, x, modv, yf, yb, g, bonus, att_lat, att_ctx, conv, gate, prm, bavg, latent_only):
    W = RW_WIDTH
    nct, nlt = geom.nct, geom.nlt
    t0 = nct if latent_only else 0
    lsel = lambda b, t: (layer, 0, 0)
    vec = lambda w: pl.BlockSpec((None, 1, w), lsel)
    wspec = lambda r, c: _const_spec((None, r, c), lsel)
    cur = lambda t: t + t0
    tok = lambda w: _tok_spec(geom, w, cur)
    mod = lambda j: _mod_spec(geom, layer, j, cur)
    attc_spec = pl.BlockSpec((TM, MLA_WIDTH), lambda b, t: (b * nct + jnp.minimum(cur(t), nct - 1), 0))
    attl_spec = pl.BlockSpec((TM, MLA_WIDTH), lambda b, t: (b * nlt + jnp.maximum(cur(t) - nct, 0), 0))
    if latent_only:
        out_spec = pl.BlockSpec((TM, D_MODEL), lambda b, t: (b * nlt + t, 0))
        out_rows = geom.B * geom.seq
    else:
        out_spec = tok(D_MODEL)
        out_rows = geom.n
    return pl.pallas_call(
        functools.partial(_mix_ffn_kernel, alpha=alpha, nct=nct, t0=t0),
        grid=(geom.B, geom.nt - t0),
        in_specs=[
            tok(D_MODEL),
            mod(2), mod(3), mod(4), mod(5),
            tok(W), tok(W),
            tok(W), tok(W), vec(W), vec(W),
            pl.BlockSpec((W, W), lambda b, t: (0, 0)),
            attc_spec, attl_spec, tok(CONV_WIDTH), tok(GATE_SEG),
            wspec(MLA_WIDTH, D_MODEL), wspec(CONV_WIDTH, D_MODEL), wspec(W, D_MODEL),
            wspec(D_MODEL, D_MODEL),
            vec(D_MODEL), vec(D_MODEL),
            wspec(D_MODEL, 2 * D_FF), wspec(D_FF, D_MODEL),
            vec(D_MODEL), vec(D_MODEL),
        ],
        out_specs=out_spec,
        out_shape=jax.ShapeDtypeStruct((out_rows, D_MODEL), F32),
        compiler_params=_params(("parallel", "parallel")),
        name="mix_ffn",
    )(x, modv, modv, modv, modv, yf, yb, g, bonus, prm["rw_gn_g"], prm["rw_gn_b"], bavg,
      att_ctx, att_lat, conv, gate,
      prm["w_o_attn"], prm["w_o_conv"], prm["w_o_rwkv"], prm["w_out"],
      prm["ln1_g"], prm["ln1_b"], prm["ffn_w13"], prm["ffn_w2"], prm["ln2_g"], prm["ln2_b"])


_ROPE_SWAP = np.concatenate([np.arange(8, 16), np.arange(0, 8), np.arange(24, 32), np.arange(16, 24)])


def _rope_tables(geom):
    pos = jnp.arange(geom.seq)
    row = (pos // GRID_W).astype(F32)
    col = (pos % GRID_W).astype(F32)
    axis_dim = MLA_ROPE // 2
    inv = ROPE_BASE ** (-jnp.arange(0, axis_dim, 2, dtype=F32) / axis_dim)
    ar, ac = row[:, None] * inv, col[:, None] * inv
    cr, sr, cc, sc = jnp.cos(ar), jnp.sin(ar), jnp.cos(ac), jnp.sin(ac)
    cos32 = jnp.concatenate([cr, cr, cc, cc], -1)
    sin32 = jnp.concatenate([-sr, sr, -sc, sc], -1)
    ones = jnp.ones((geom.seq, MLA_NOPE), F32)
    zpad = jnp.zeros((geom.seq, HEAD_PAD - MLA_NOPE - MLA_ROPE), F32)
    cos_l = jnp.concatenate([ones, cos32, zpad], -1)
    sin_l = jnp.concatenate([jnp.zeros_like(ones), sin32, zpad], -1)
    cos_c = jnp.ones((geom.ctx_len, HEAD_PAD), F32)
    sin_c = jnp.zeros((geom.ctx_len, HEAD_PAD), F32)
    return jnp.concatenate([cos_c, cos_l], 0), jnp.concatenate([sin_c, sin_l], 0)


def _layout_weights(w_in, w_uq, w_ukv):
    L = w_in.shape[0]
    o_q, o_kv, o_kr = MLA_Q_LORA, MLA_Q_LORA + MLA_KV_LORA, MLA_Q_LORA + MLA_KV_LORA + MLA_ROPE
    krope = w_in[:, :, o_kv:o_kr]
    zl = jnp.zeros((L, D_MODEL, MLA_NOPE), F32)
    zr = jnp.zeros((L, D_MODEL, HEAD_PAD - MLA_NOPE - MLA_ROPE), F32)
    w_in_p = jnp.concatenate(
        [w_in[:, :, :o_kv], zl, krope, zr, zl, krope[:, :, _ROPE_SWAP], zr, w_in[:, :, o_kr:]], -1).astype(BF16)

    wq = w_uq.reshape(L, MLA_Q_LORA, MLA_HEADS, MLA_NOPE + MLA_ROPE)
    q_rope = wq[..., MLA_NOPE:]
    zq = jnp.zeros((L, MLA_Q_LORA, MLA_HEADS, HEAD_PAD - MLA_NOPE - MLA_ROPE), F32)
    zn = jnp.zeros((L, MLA_Q_LORA, MLA_HEADS, MLA_NOPE), F32)
    wq_p = jnp.concatenate([wq, zq], -1).reshape(L, MLA_Q_LORA, -1).astype(BF16)
    wqs_p = jnp.concatenate([zn, q_rope[..., _ROPE_SWAP], zq], -1).reshape(L, MLA_Q_LORA, -1).astype(BF16)

    wkv = w_ukv.reshape(L, MLA_KV_LORA, MLA_HEADS, MLA_NOPE + MLA_V)
    zk = jnp.zeros((L, MLA_KV_LORA, MLA_HEADS, HEAD_PAD - MLA_NOPE), F32)
    wk_p = jnp.concatenate([wkv[..., :MLA_NOPE], zk], -1).reshape(L, MLA_KV_LORA, -1).astype(BF16)
    zv = jnp.zeros((L, MLA_KV_LORA, MLA_HEADS, HEAD_PAD - MLA_V), F32)
    wv_p = jnp.concatenate([wkv[..., MLA_NOPE:], zv], -1).reshape(L, MLA_KV_LORA, -1).astype(BF16)
    return w_in_p, wq_p, wqs_p, wk_p, wv_p


def _head_block_ones():
    idx = np.arange(RW_WIDTH) // RW_HEAD
    return jnp.asarray((idx[:, None] == idx[None, :]).astype(np.float32), dtype=BF16)


def kernel(x, c, ctx, c_ctx, mod_w, mod_b, w_in, q_norm, w_uq, kv_norm, w_ukv, w_o_attn, conv_w, w_o_conv,
           rw_mu, rw_w0, rw_w_up, rw_a0, rw_a_up, rw_g_up, rw_k_k, rw_k_a, rw_r_k, rw_gn_g, rw_gn_b,
           w_o_rwkv, w_out, ln1_g, ln1_b, ffn_w13, ffn_w2, ln2_g, ln2_b):
    B, seq, _ = x.shape
    ctx_len = ctx.shape[1]
    L = mod_w.shape[0]
    geom = _Geom(B, ctx_len, seq)
    alpha = (2.0 * L) ** 0.25

    rows = -(-(B + 1) // HALO) * HALO
    c_all = jnp.concatenate([c, c_ctx[None, :], jnp.zeros((rows - B - 1, D_MODEL), F32)], 0)
    modv = _mod_vectors(c_all, mod_w, mod_b).reshape(L, rows, 1, -1)

    w_in_p, wq_p, wqs_p, wk_p, wv_p = _layout_weights(w_in, w_uq, w_ukv)
    cos_t, sin_t = _rope_tables(geom)
    bsum = _head_block_ones()
    vec3 = lambda a: a.reshape(L, 1, -1)
    prm = dict(
        rw_mu=vec3(rw_mu), rw_w0=rw_w0, rw_w_up=rw_w_up, rw_a0=rw_a0, rw_a_up=rw_a_up, rw_g_up=rw_g_up,
        rw_k_k=vec3(rw_k_k), rw_k_a=vec3(rw_k_a), rw_r_k=vec3(rw_r_k), conv_w=conv_w,
        rw_gn_g=vec3(rw_gn_g), rw_gn_b=vec3(rw_gn_b),
        w_o_attn=w_o_attn.astype(BF16), w_o_conv=w_o_conv.astype(BF16), w_o_rwkv=w_o_rwkv.astype(BF16),
        w_out=w_out.astype(BF16), ln1_g=vec3(ln1_g), ln1_b=vec3(ln1_b),
        ffn_w13=ffn_w13.astype(BF16), ffn_w2=ffn_w2.astype(BF16), ln2_g=vec3(ln2_g), ln2_b=vec3(ln2_b),
    )
    qn, kvn = vec3(q_norm), vec3(kv_norm)

    xs = jnp.concatenate([ctx.reshape(geom.nc, D_MODEL), x.reshape(B * seq, D_MODEL)], axis=0)
    for l in range(L):
        mla, conv, rw, gate = _in_proj(geom, l, xs, modv, w_in_p)
        q, k, v = _mla_proj(geom, l, mla, qn, kvn, wq_p, wqs_p, wk_p, wv_p, cos_t, sin_t)
        att_lat, att_ctx = _attention(geom, q, k, v)
        g, bonus, cv, gm, hm, rq, y0 = _features(geom, l, rw, conv, prm, bsum)
        yf, yb = _wkv_scan(geom, gm, hm, rq, y0)
        xs = _mix_ffn(geom, l, alpha, xs, modv, yf, yb, g, bonus, att_lat, att_ctx, cv, gate, prm, bsum,
                      latent_only=(l == L - 1))
    return xs.reshape(B, seq, D_MODEL)
```

```python
import functools

import numpy as np
import jax
import jax.numpy as jnp
from jax import lax
from jax.experimental import pallas as pl
from jax.experimental.pallas import tpu as pltpu

F32 = jnp.float32
BF16 = jnp.bfloat16

D_MODEL = 1024
GRID_W = 64
MLA_HEADS = 8
MLA_Q_LORA = 384
MLA_KV_LORA = 256
MLA_NOPE = 64
MLA_ROPE = 32
MLA_V = 64
MLA_WIDTH = MLA_HEADS * MLA_V
ROPE_BASE = 10000.0
CONV_WIDTH = 512
RW_HEADS = 8
RW_HEAD = 64
RW_WIDTH = RW_HEADS * RW_HEAD
RW_DECAY_LORA = 64
RW_ICLR_LORA = 64
RW_GATE_LORA = 128
RW_GN_EPS = 64e-5
RW_IN = 3 * RW_WIDTH + 2 * RW_DECAY_LORA + 2 * RW_ICLR_LORA + RW_GATE_LORA
N_BRANCH = 3
D_FF = 2816
LN_EPS = 1e-5
RMS_EPS = 1e-6

LANES = 128
HEAD_PAD = LANES
TM = 256
TM_BIG = 512
CHUNK = 64
PAIR = 2 * RW_HEAD
assert PAIR == LANES
PREP_INTERLEAVE = 4
HALO = 16
VMEM_LIMIT = 56 * 1024 * 1024

MLA_SEG = MLA_Q_LORA + MLA_KV_LORA + 2 * HEAD_PAD
CONV_SEG = 3 * CONV_WIDTH
GATE_SEG = N_BRANCH * D_MODEL
IN_SEGS = (MLA_SEG, CONV_SEG, RW_IN, GATE_SEG)
IN_OFFS = tuple(int(v) for v in np.cumsum((0,) + IN_SEGS))


def _bdot(a, b):
    return jnp.dot(a.astype(BF16), b.astype(BF16), preferred_element_type=F32)


def _split2(a):
    hi = a.astype(BF16)
    lo = (a - hi.astype(F32)).astype(BF16)
    return hi, lo


def _dot3(a, b, dims=None):
    ah, al = _split2(a)
    bh, bl = _split2(b)
    if dims is None:
        f = lambda u, v: jnp.dot(u, v, preferred_element_type=F32)
    else:
        f = lambda u, v: lax.dot_general(u, v, dims, preferred_element_type=F32)
    return f(ah, bh) + (f(ah, bl) + f(al, bh))


def _dot1(a, b, dims=None):
    a, b = a.astype(BF16), b.astype(BF16)
    if dims is None:
        return jnp.dot(a, b, preferred_element_type=F32)
    return lax.dot_general(a, b, dims, preferred_element_type=F32)


_dot_wkv = _dot1


def _dot_exact_rhs(a, b_bf16):
    a1, a2 = _split2(a)
    return jnp.dot(a1, b_bf16, preferred_element_type=F32) + jnp.dot(a2, b_bf16, preferred_element_type=F32)


def _sigmoid(x):
    return 0.5 * jnp.tanh(0.5 * x) + 0.5


def _layer_norm(x, g, b):
    mu = jnp.mean(x, -1, keepdims=True)
    xc = x - mu
    var = jnp.mean(xc * xc, -1, keepdims=True)
    return xc * lax.rsqrt(var + LN_EPS) * g + b


def _const_spec(shape, index_map):
    return pl.BlockSpec(shape, index_map, pipeline_mode=pl.Buffered(1))


def _params(sem):
    return pltpu.CompilerParams(dimension_semantics=sem, vmem_limit_bytes=VMEM_LIMIT)


def _mod_kernel(c_ref, w_ref, b_ref, o_ref):
    c = c_ref[...]
    s = c * _sigmoid(c)
    o_ref[...] = _dot3(s, w_ref[...]) + b_ref[...]


def _mod_vectors(c_all, mod_w, mod_b):
    L = mod_w.shape[0]
    R = c_all.shape[0]
    n = mod_w.shape[2] // D_MODEL
    return pl.pallas_call(
        _mod_kernel,
        grid=(L, n),
        in_specs=[
            pl.BlockSpec((R, D_MODEL), lambda l, j: (0, 0)),
            pl.BlockSpec((None, D_MODEL, D_MODEL), lambda l, j: (l, 0, j)),
            pl.BlockSpec((None, 1, D_MODEL), lambda l, j: (l, 0, j)),
        ],
        out_specs=pl.BlockSpec((None, R, D_MODEL), lambda l, j: (l, 0, j)),
        out_shape=jax.ShapeDtypeStruct((L, R, n * D_MODEL), F32),
        compiler_params=_params(("parallel", "parallel")),
        name="mod_vectors",
    )(c_all, mod_w, mod_b.reshape(L, 1, -1))


class _Geom:
    def __init__(self, B, ctx_len, seq):
        assert ctx_len % TM == 0 and seq % TM_BIG == 0 and seq % GRID_W == 0
        self.B, self.ctx_len, self.seq = B, ctx_len, seq
        self.tt = ctx_len + seq
        self.nct = ctx_len // TM
        self.nlt = seq // TM
        self.nt = self.nct + self.nlt
        self.nc = B * ctx_len
        self.n = B * self.tt
        assert self.nc % TM_BIG == 0 and self.nc % seq == 0

    def tile(self, b, t):
        return jnp.where(t < self.nct, b * self.nct + t, self.B * self.nct + b * self.nlt + (t - self.nct))


def _tok_spec(geom, width):
    return pl.BlockSpec((TM, width), lambda b, t: (geom.tile(b, t), 0))


def _in_proj_kernel(x_ref, sh_ref, sc_ref, w_ref, mla_ref, conv_ref, rw_ref, gate_ref):
    h = (x_ref[...] * (1.0 + sc_ref[...]) + sh_ref[...]).astype(BF16)
    for o_ref, lo, hi in zip((mla_ref, conv_ref, rw_ref, gate_ref), IN_OFFS[:-1], IN_OFFS[1:]):
        o_ref[...] = jnp.dot(h, w_ref[:, lo:hi], preferred_element_type=F32).astype(BF16)


def _in_proj(geom, layer, x, modv, w_in_p):
    ctx_tiles = geom.nc // TM_BIG
    per_row = geom.seq // TM_BIG

    def mod(j):
        row = lambda i: jnp.where(i < ctx_tiles, geom.B, (i - ctx_tiles) // per_row)
        return pl.BlockSpec((None, None, 1, D_MODEL), lambda i: (layer, row(i), 0, j))

    big = lambda w: pl.BlockSpec((TM_BIG, w), lambda i: (i, 0))
    return pl.pallas_call(
        _in_proj_kernel,
        grid=(geom.n // TM_BIG,),
        in_specs=[big(D_MODEL), mod(0), mod(1),
                  _const_spec((None, D_MODEL, IN_OFFS[-1]), lambda i: (layer, 0, 0))],
        out_specs=[big(w) for w in IN_SEGS],
        out_shape=[jax.ShapeDtypeStruct((geom.n, w), BF16) for w in IN_SEGS],
        compiler_params=_params(("parallel",)),
        name="in_proj",
    )(x, modv, modv, w_in_p)


def _mla_kernel(m_ref, qn_ref, kvn_ref, wq_ref, wqs_ref, wk_ref, wv_ref, cos_ref, sin_ref,
                q_ref, k_ref, v_ref):
    m = m_ref[...].astype(F32)
    cq = m[:, :MLA_Q_LORA]
    ckv = m[:, MLA_Q_LORA:MLA_Q_LORA + MLA_KV_LORA]
    kslab = m[:, MLA_Q_LORA + MLA_KV_LORA:MLA_Q_LORA + MLA_KV_LORA + HEAD_PAD]
    kslab_sw = m[:, MLA_Q_LORA + MLA_KV_LORA + HEAD_PAD:]
    cqn = (cq * lax.rsqrt(jnp.mean(cq * cq, -1, keepdims=True) + RMS_EPS) * qn_ref[...]).astype(BF16)
    ckvn = (ckv * lax.rsqrt(jnp.mean(ckv * ckv, -1, keepdims=True) + RMS_EPS) * kvn_ref[...]).astype(BF16)
    cos_t = cos_ref[...]
    sin_t = sin_ref[...]
    qa = jnp.dot(cqn, wq_ref[...], preferred_element_type=F32)
    qb = jnp.dot(cqn, wqs_ref[...], preferred_element_type=F32)
    kn = jnp.dot(ckvn, wk_ref[...], preferred_element_type=F32)
    kr = kslab * cos_t + kslab_sw * sin_t
    scale = (MLA_NOPE + MLA_ROPE) ** -0.5
    for h in range(MLA_HEADS):
        sl = slice(h * HEAD_PAD, (h + 1) * HEAD_PAD)
        q_ref[:, sl] = ((qa[:, sl] * cos_t + qb[:, sl] * sin_t) * scale).astype(BF16)
        k_ref[:, sl] = (kn[:, sl] + kr).astype(BF16)
    lane = lax.broadcasted_iota(jnp.int32, (1, MLA_HEADS * HEAD_PAD), 1) % HEAD_PAD
    v_ref[...] = (jnp.dot(ckvn, wv_ref[...], preferred_element_type=F32)
                  + (lane == MLA_V).astype(F32)).astype(BF16)


def _mla_proj(geom, layer, mla, q_norm, kv_norm, wq, wqs, wk, wv, cos_t, sin_t):
    hw = MLA_HEADS * HEAD_PAD
    lsel = lambda i: (layer, 0, 0)
    ctx_tiles = geom.nc // TM_BIG
    per_row = geom.seq // TM_BIG
    big = lambda w: pl.BlockSpec((TM_BIG, w), lambda i: (i, 0))
    rope = pl.BlockSpec((TM_BIG, HEAD_PAD), lambda i: (jnp.where(i < ctx_tiles, 0, 1 + (i - ctx_tiles) % per_row), 0))
    return pl.pallas_call(
        _mla_kernel,
        grid=(geom.n // TM_BIG,),
        in_specs=[
            big(MLA_SEG),
            pl.BlockSpec((None, 1, MLA_Q_LORA), lsel),
            pl.BlockSpec((None, 1, MLA_KV_LORA), lsel),
            _const_spec((None, MLA_Q_LORA, hw), lsel),
            _const_spec((None, MLA_Q_LORA, hw), lsel),
            _const_spec((None, MLA_KV_LORA, hw), lsel),
            _const_spec((None, MLA_KV_LORA, hw), lsel),
            rope, rope,
        ],
        out_specs=[big(hw)] * 3,
        out_shape=[jax.ShapeDtypeStruct((geom.n, hw), BF16)] * 3,
        compiler_params=_params(("parallel",)),
        name="mla_proj",
    )(mla, q_norm, kv_norm, wq, wqs, wk, wv, cos_t, sin_t)


HEADS_PER_STEP = 4


def _attend(q_ref, kv_refs, o_ref):
    heads = [slice(j * HEAD_PAD, (j + 1) * HEAD_PAD) for j in range(HEADS_PER_STEP)]

    def scores(hs):
        q = q_ref[:, hs]
        return [lax.dot_general(q, k_ref[:, hs], (((1,), (1,)), ((), ())), preferred_element_type=F32)
                for k_ref, _ in kv_refs]

    def probs(ss):
        m = functools.reduce(jnp.maximum, [jnp.max(s, -1, keepdims=True) for s in ss])
        return [jnp.exp(s - m).astype(BF16) for s in ss]

    def weighted(ps, hs):
        pv = sum(jnp.dot(p, v_ref[:, hs], preferred_element_type=F32) for p, (_, v_ref) in zip(ps, kv_refs))
        return pv[:, :MLA_V] / pv[:, MLA_V:MLA_V + 1]

    n = len(heads)
    ss = [scores(heads[0]), scores(heads[1])] + [None] * (n - 2)
    ps = [probs(ss[0])] + [None] * (n - 1)
    outs = []
    for j in range(n):
        if j + 2 < n:
            ss[j + 2] = scores(heads[j + 2])
        if j + 1 < n:
            ps[j + 1] = probs(ss[j + 1])
        outs.append(weighted(ps[j], heads[j]))
    o_ref[...] = jnp.concatenate(outs, -1).astype(BF16)


def _attn_kernel(ql_ref, qc_ref, kc_ref, vc_ref, kl_ref, vl_ref, ol_ref, oc_ref):
    _attend(ql_ref, [(kc_ref, vc_ref), (kl_ref, vl_ref)], ol_ref)

    @pl.when(pl.program_id(2) == 0)
    def _():
        _attend(qc_ref, [(kc_ref, vc_ref)], oc_ref)


def _attention(geom, q, k, v):
    qw = HEADS_PER_STEP * HEAD_PAD
    vw = HEADS_PER_STEP * MLA_V
    per_row = geom.seq // TM_BIG
    lat0 = geom.nc // TM_BIG
    seq0 = geom.nc // geom.seq
    ctx_spec = pl.BlockSpec((geom.ctx_len, qw), lambda b, h, t: (b, h))
    lat_spec = pl.BlockSpec((geom.seq, qw), lambda b, h, t: (seq0 + b, h))
    return pl.pallas_call(
        _attn_kernel,
        grid=(geom.B, MLA_HEADS // HEADS_PER_STEP, per_row),
        in_specs=[
            pl.BlockSpec((TM_BIG, qw), lambda b, h, t: (lat0 + b * per_row + t, h)),
            ctx_spec, ctx_spec, ctx_spec, lat_spec, lat_spec,
        ],
        out_specs=[pl.BlockSpec((TM_BIG, vw), lambda b, h, t: (b * per_row + t, h)),
                   pl.BlockSpec((geom.ctx_len, vw), lambda b, h, t: (b, h))],
        out_shape=[jax.ShapeDtypeStruct((geom.B * geom.seq, MLA_WIDTH), BF16),
                   jax.ShapeDtypeStruct((geom.nc, MLA_WIDTH), BF16)],
        compiler_params=_params(("parallel", "parallel", "arbitrary")),
        name="attn",
    )(q, q, k, v, k, v)


def _shift_prev(x, halo_row):
    rows = lax.broadcasted_iota(jnp.int32, x.shape, 0)
    return jnp.where(rows == 0, halo_row, pltpu.roll(x, 1, 0))


def _shift_next(x, halo_row):
    rows = lax.broadcasted_iota(jnp.int32, x.shape, 0)
    return jnp.where(rows == x.shape[0] - 1, halo_row, pltpu.roll(x, x.shape[0] - 1, 0))


def _feat_kernel(rw_ref, rwp_ref, rwn_ref, cv_ref, cvp_ref, cvn_ref,
                 mu_ref, w0_ref, wup_ref, a0_ref, aup_ref, gup_ref, kk_ref, ka_ref, rk_ref,
                 cw_ref, bsum_ref,
                 g_o, bonus_o, conv_o, gm_o, hm_o, rq_o, y0_o, *, nct, nt):
    t = pl.program_id(1)
    first = jnp.logical_or(t == 0, t == nct).astype(F32)
    last = jnp.logical_or(t == nct - 1, t == nt - 1).astype(F32)
    keep_prev = 1.0 - first
    keep_next = 1.0 - last
    W = RW_WIDTH

    def gated(ref):
        blk = ref[...].astype(F32)
        return blk[..., 2 * CONV_WIDTH:] * blk[..., :CONV_WIDTH]
    cblk = cv_ref[...].astype(F32)
    up = cblk[:, 2 * CONV_WIDTH:] * cblk[:, :CONV_WIDTH]
    up_p = _shift_prev(up, gated(cvp_ref)[HALO - 1:HALO] * keep_prev)
    up_n = _shift_next(up, gated(cvn_ref)[0:1] * keep_next)
    cw = cw_ref[...]
    conv_o[...] = (cblk[:, CONV_WIDTH:2 * CONV_WIDTH]
                   * (up_p * cw[0:1] + up * cw[1:2] + up_n * cw[2:3])).astype(BF16)

    p = rw_ref[...].astype(F32)
    p_prev = _shift_prev(p, rwp_ref[HALO - 1:HALO, :].astype(F32) * keep_prev)
    p_next = _shift_next(p, rwn_ref[0:1, :].astype(F32) * keep_next)
    p = p + (0.5 * (p_prev + p_next) - p) * mu_ref[...]
    r = p[:, 0:W]
    k = p[:, W:2 * W]
    v = p[:, 2 * W:3 * W]
    wd = p[:, 3 * W:3 * W + 2 * RW_DECAY_LORA]
    ad = p[:, 3 * W + 2 * RW_DECAY_LORA:3 * W + 2 * RW_DECAY_LORA + 2 * RW_ICLR_LORA]
    gd = p[:, 3 * W + 2 * RW_DECAY_LORA + 2 * RW_ICLR_LORA:]

    bsum = bsum_ref[...]
    kkf = k * kk_ref[...]
    kk = kkf * lax.rsqrt(jnp.maximum(_dot_exact_rhs(kkf * kkf, bsum), 1e-24))
    g_o[...] = _dot3(_sigmoid(gd), gup_ref[...]).astype(BF16)
    bonus_o[...] = (_dot_exact_rhs(r * k * rk_ref[...], bsum) * v).astype(BF16)
    for z in range(2):
        wdz = jnp.tanh(wd[:, z * RW_DECAY_LORA:(z + 1) * RW_DECAY_LORA])
        adz = ad[:, z * RW_ICLR_LORA:(z + 1) * RW_ICLR_LORA]
        tz = w0_ref[z:z + 1, :] + _dot3(wdz, wup_ref[z])
        lw = _sigmoid(tz) * (-float(np.exp(-0.5)))
        a = _sigmoid(a0_ref[z:z + 1, :] + _dot3(adz, aup_ref[z]))
        kd = k * (1.0 + (a - 1.0) * ka_ref[...])
        _wkv_chunk_reduce_pairs(z, r, v, kk, kd, kk * a, lw, gm_o, hm_o, rq_o, y0_o)


def _features(geom, layer, rw, conv, prm, bsum):
    nt, n = geom.nt, geom.n
    blocks_per_tile = TM // HALO
    last_halo_block = n // HALO - 1

    def prev_spec(w):
        return pl.BlockSpec((HALO, w), lambda b, t: (jnp.maximum(geom.tile(b, t) * blocks_per_tile - 1, 0), 0))

    def next_spec(w):
        return pl.BlockSpec(
            (HALO, w), lambda b, t: (jnp.minimum((geom.tile(b, t) + 1) * blocks_per_tile, last_halo_block), 0))

    lsel2 = lambda b, t: (layer, 0, 0)
    lsel3 = lambda b, t: (layer, 0, 0, 0)
    W = RW_WIDTH
    tok = _tok_spec(geom, W)
    dir_spec = pl.BlockSpec((2, TM, W), lambda b, t: (0, geom.tile(b, t), 0))
    tok_shape = jax.ShapeDtypeStruct((n, W), BF16)
    dir_shape = jax.ShapeDtypeStruct((2, n, W), BF16)
    return pl.pallas_call(
        functools.partial(_feat_kernel, nct=geom.nct, nt=nt),
        grid=(geom.B, nt),
        in_specs=[
            _tok_spec(geom, RW_IN), prev_spec(RW_IN), next_spec(RW_IN),
            _tok_spec(geom, CONV_SEG), prev_spec(CONV_SEG), next_spec(CONV_SEG),
            pl.BlockSpec((None, 1, RW_IN), lsel2),
            pl.BlockSpec((None, 2, W), lsel2),
            pl.BlockSpec((None, 2, RW_DECAY_LORA, W), lsel3),
            pl.BlockSpec((None, 2, W), lsel2),
            pl.BlockSpec((None, 2, RW_ICLR_LORA, W), lsel3),
            pl.BlockSpec((None, RW_GATE_LORA, W), lsel2),
            pl.BlockSpec((None, 1, W), lsel2),
            pl.BlockSpec((None, 1, W), lsel2),
            pl.BlockSpec((None, 1, W), lsel2),
            pl.BlockSpec((None, 3, CONV_WIDTH), lsel2),
            pl.BlockSpec((W, W), lambda b, t: (0, 0)),
        ],
        out_specs=[tok] * 3 + [dir_spec] * 4,
        out_shape=[tok_shape] * 3 + [dir_shape] * 4,
        compiler_params=_params(("parallel", "parallel")),
        name="feat_wkv_prep",
    )(rw, rw, rw, conv, conv, conv,
      prm["rw_mu"], prm["rw_w0"], prm["rw_w_up"], prm["rw_a0"], prm["rw_a_up"], prm["rw_g_up"],
      prm["rw_k_k"], prm["rw_k_a"], prm["rw_r_k"], prm["conv_w"], bsum)


def _block_diag(x, lo, hi):
    zero = jnp.zeros_like(x)
    return jnp.concatenate([jnp.where(lo, x, zero), jnp.where(hi, x, zero)], axis=0)


def _wkv_chunk_reduce_pairs(z, r_all, v_all, kk_all, kd_all, b_all, lw_all, g_o, h_o, rq_o, y0_o):
    fwd = z == 0
    C = CHUNK
    assert C == RW_HEAD
    ri = lax.broadcasted_iota(jnp.int32, (C, C), 0)
    ci = lax.broadcasted_iota(jnp.int32, (C, C), 1)
    incl_bf = ((ci <= ri) if fwd else (ci >= ri)).astype(F32).astype(BF16)
    rp = lax.broadcasted_iota(jnp.int32, (C, PAIR), 0)
    cp = lax.broadcasted_iota(jnp.int32, (C, PAIR), 1) % C
    strict = cp < rp if fwd else cp > rp
    incl = cp <= rp if fwd else cp >= rp
    eye = (cp == rp).astype(F32)
    lane1 = lax.broadcasted_iota(jnp.int32, (1, PAIR), 1)
    lane2 = lax.broadcasted_iota(jnp.int32, (1, 2 * PAIR), 1) % PAIR
    lo1, hi1 = lane1 < RW_HEAD, lane1 >= RW_HEAD
    lo2, hi2 = lane2 < RW_HEAD, lane2 >= RW_HEAD
    lane_even = lax.broadcasted_iota(jnp.int32, (1, RW_WIDTH), 1) % PAIR < RW_HEAD
    r2 = lax.broadcasted_iota(jnp.int32, (PAIR, PAIR), 0)
    c2 = lax.broadcasted_iota(jnp.int32, (PAIR, PAIR), 1)
    eye2 = r2 == c2
    nt_dims = (((1,), (1,)), ((), ()))
    npair = RW_WIDTH // PAIR
    pairs = [slice(p * PAIR, (p + 1) * PAIR) for p in range(npair)]

    def prologue(c):
        rows = slice(c * C, (c + 1) * C)
        lw = lw_all[rows, :]
        L = _dot_exact_rhs_lhs(incl_bf, lw)
        Ltot = L[C - 1:C, :] if fwd else L[0:1, :]
        enL = jnp.exp(-L)
        etail = jnp.exp(Ltot - L)
        kd = kd_all[rows, :]
        bb = b_all[rows, :]
        rt = r_all[rows, :] * jnp.exp(L)
        bt = (bb * enL).astype(BF16)
        kt = (kd * enL).astype(BF16)
        zero = jnp.zeros_like(bt)
        return dict(
            c=c, rows=rows, rt=rt,
            lhs=jnp.concatenate([kk_all[rows, :] * jnp.exp(L - lw), rt], axis=0).astype(BF16),
            rhs=jnp.concatenate([jnp.where(lane_even, bt, zero), jnp.where(lane_even, zero, bt),
                                 jnp.where(lane_even, kt, zero), jnp.where(lane_even, zero, kt)], axis=0),
            ktp_t=jnp.transpose(kd * etail).astype(BF16),
            btp_t=jnp.transpose(bb * etail).astype(BF16),
            ptot=jnp.exp(Ltot),
            vv=v_all[rows, :].astype(BF16))

    for c0 in range(0, TM // C, PREP_INTERLEAVE):
        chunks = [prologue(c) for c in range(c0, c0 + PREP_INTERLEAVE)]
        items = [(ch, ps) for ch in chunks for ps in pairs]
        aa = [_dot_wkv(ch["lhs"][:, ps], ch["rhs"][:, ps], nt_dims) for ch, ps in items]
        a_b = [jnp.where(strict, m[:C, :PAIR], 0.0) for m in aa]
        a_kk = [jnp.concatenate([jnp.where(strict, m[:C, PAIR:], 0.0), jnp.where(incl, m[C:, PAIR:], 0.0)], axis=0)
                .astype(BF16) for m in aa]
        a_rb = [jnp.where(incl, m[C:, :PAIR], 0.0).astype(BF16) for m in aa]
        bd = lambda p: _block_diag(p.astype(BF16), lo1, hi1)
        pw = [_dot_wkv(a, bd(a)) for a in a_b]
        xs = [eye - a for a in a_b]
        rounds = int(np.log2(C)) - 1
        for i in range(rounds):
            if i + 1 < rounds:
                both = [_dot_wkv(jnp.concatenate([p, x], axis=0), bd(p)) for p, x in zip(pw, xs)]
                pw = [m[:C] for m in both]
                xs = [x + m[C:] for x, m in zip(xs, both)]
            else:
                xs = [x + _dot_wkv(x, bd(p)) for x, p in zip(xs, pw)]
        avv = [_dot_wkv(m, _block_diag(ch["vv"][:, ps], lo1, hi1)) for m, (ch, ps) in zip(a_kk, items)]
        wu = [_dot_wkv(t, _block_diag(jnp.concatenate([ch["lhs"][:C, ps], av[:C].astype(BF16)], axis=1), lo2, hi2))
              .astype(BF16) for t, av, (ch, ps) in zip(xs, avv, items)]
        rbwu = [_dot_wkv(m, _block_diag(x, lo2, hi2)) for m, x in zip(a_rb, wu)]
        bwu = [_dot_wkv(ch["btp_t"][ps, :], x) for x, (ch, ps) in zip(wu, items)]
        kv = [_dot_wkv(ch["ktp_t"][ps, :], ch["vv"][:, ps]) for ch, ps in items]
        for av, rb, bw, kvp, (ch, ps) in zip(avv, rbwu, bwu, kv, items):
            rows = ch["rows"]
            rq_o[z, rows, ps] = (ch["rt"][:, ps] - rb[:, :PAIR]).astype(BF16)
            y0_o[z, rows, ps] = (av[C:] - rb[:, PAIR:]).astype(BF16)
            gfull = jnp.where(eye2, ch["ptot"][:, ps], 0.0) - bw[:, :PAIR]
            hfull = kvp - bw[:, PAIR:]
            g_o[z, rows, ps] = jnp.where(lo1, gfull[:C], gfull[C:]).astype(BF16)
            h_o[z, rows, ps] = jnp.where(lo1, hfull[:C], hfull[C:]).astype(BF16)


def _dot_exact_rhs_lhs(mask_bf16, x):
    x1, x2 = _split2(x)
    return jnp.dot(mask_bf16, x1, preferred_element_type=F32) + jnp.dot(mask_bf16, x2, preferred_element_type=F32)


def _wkv_scan_kernel(gf_ref, hf_ref, rqf_ref, y0f_ref, gb_ref, hb_ref, rqb_ref, y0b_ref,
                     yf_o, yb_o, state):
    C = CHUNK
    nch = TM // C

    @pl.when(pl.program_id(1) == 0)
    def _():
        state[...] = jnp.zeros_like(state)

    pairs = [slice(p * PAIR, (p + 1) * PAIR) for p in range(RW_WIDTH // PAIR)]
    lane = lax.broadcasted_iota(jnp.int32, (1, RW_WIDTH), 1) % PAIR
    lo, hi = lane < RW_HEAD, lane >= RW_HEAD
    dirs = ((gf_ref, hf_ref, rqf_ref, y0f_ref, yf_o), (gb_ref, hb_ref, rqb_ref, y0b_ref, yb_o))
    m = [state[0], state[1]]
    for c in range(nch):
        for z, (g_ref, h_ref, rq_ref, y0_ref, y_o) in enumerate(dirs):
            cc = c if z == 0 else nch - 1 - c
            rows = slice(cc * C, (cc + 1) * C)
            g = _block_diag(g_ref[rows, :], lo, hi)
            h = _block_diag(h_ref[rows, :], lo, hi)
            lhs = jnp.concatenate([rq_ref[rows, :], g], axis=0)
            mb = m[z].astype(BF16)
            ym = jnp.concatenate([jnp.dot(lhs[:, ps], mb[:, ps], preferred_element_type=F32) for ps in pairs], -1)
            y_o[rows, :] = (ym[:C] + y0_ref[rows, :]).astype(y_o.dtype)
            m[z] = ym[C:] + h
    state[0] = m[0]
    state[1] = m[1]


def _wkv_scan(geom, g, h, rq, y0):
    W = RW_WIDTH
    nt, nct = geom.nt, geom.nct

    def fwd_tile(b, t):
        return (0, geom.tile(b, t), 0)

    def bwd_tile(b, t):
        return (1, geom.tile(b, jnp.where(t < nct, nct - 1 - t, nt - 1 - (t - nct))), 0)

    fspec = pl.BlockSpec((None, TM, W), fwd_tile)
    bspec = pl.BlockSpec((None, TM, W), bwd_tile)
    return pl.pallas_call(
        _wkv_scan_kernel,
        grid=(geom.B, nt),
        in_specs=[fspec] * 4 + [bspec] * 4,
        out_specs=[pl.BlockSpec((TM, W), lambda b, t: fwd_tile(b, t)[1:]),
                   pl.BlockSpec((TM, W), lambda b, t: bwd_tile(b, t)[1:])],
        out_shape=[jax.ShapeDtypeStruct((geom.n, W), BF16)] * 2,
        scratch_shapes=[pltpu.VMEM((2, PAIR, W), F32)],
        compiler_params=_params(("parallel", "arbitrary")),
        name="wkv_scan",
    )(g, h, rq, y0, g, h, rq, y0)


def _mix_ffn_kernel(x_ref, g1_ref, sh2_ref, sc2_ref, g2_ref,
                    yf_ref, yb_ref, gg_ref, bonus_ref, gng_ref, gnb_ref, bavg_ref,
                    attc_ref, attl_ref, conv_ref, gate_ref,
                    woa_ref, woc_ref, wor_ref, wout_ref,
                    l1g_ref, l1b_ref, w13_ref, w2_ref, l2g_ref, l2b_ref,
                    o_ref, *, alpha, nct, t0):
    y = yf_ref[...].astype(F32) + yb_ref[...].astype(F32)
    bavg = bavg_ref[...]
    mu = _dot_exact_rhs(y, bavg) * (1.0 / RW_HEAD)
    yc = y - mu
    var = _dot_exact_rhs(yc * yc, bavg) * (1.0 / RW_HEAD)
    yn = yc * lax.rsqrt(var + RW_GN_EPS) * gng_ref[...] + gnb_ref[...]
    rwo = (yn + bonus_ref[...]) * gg_ref[...]

    gl = gate_ref[...].astype(F32)
    att = jnp.where(pl.program_id(0) + t0 < nct, attc_ref[...], attl_ref[...])
    merged = (_sigmoid(gl[:, 0:D_MODEL]) * _bdot(att, woa_ref[...])
              + _sigmoid(gl[:, D_MODEL:2 * D_MODEL]) * _bdot(conv_ref[...], woc_ref[...])
              + _sigmoid(gl[:, 2 * D_MODEL:]) * _bdot(rwo, wor_ref[...]))
    o = _bdot(merged, wout_ref[...])
    x = x_ref[...]
    x1 = _layer_norm(alpha * x + g1_ref[...] * o, l1g_ref[...], l1b_ref[...])

    hmod = x1 * (1.0 + sc2_ref[...]) + sh2_ref[...]
    ug = _bdot(hmod, w13_ref[...])
    u = ug[:, :D_FF]
    f = _bdot(u * _sigmoid(u) * ug[:, D_FF:], w2_ref[...])
    o_ref[...] = _layer_norm(alpha * x1 + g2_ref[...] * f, l2g_ref[...], l2b_ref[...])


def _mix_ffn(geom, layer, alpha, x, modv, yf, yb, g, bonus, att_lat, att_ctx, conv, gate, prm, bavg, latent_only):
    W = RW_WIDTH
    ctx_tiles = geom.nc // TM_BIG
    per_row = geom.seq // TM_BIG
    t0 = ctx_tiles if latent_only else 0
    lsel = lambda i: (layer, 0, 0)
    vec = lambda w: pl.BlockSpec((None, 1, w), lsel)
    wspec = lambda r, c: _const_spec((None, r, c), lsel)
    tok = lambda w: pl.BlockSpec((TM_BIG, w), lambda i: (i + t0, 0))
    mod_row = lambda i: jnp.where(i + t0 < ctx_tiles, geom.B, (i + t0 - ctx_tiles) // per_row)
    mod = lambda j: pl.BlockSpec((None, None, 1, D_MODEL), lambda i: (layer, mod_row(i), 0, j))
    attc_spec = pl.BlockSpec((TM_BIG, MLA_WIDTH), lambda i: (jnp.minimum(i + t0, ctx_tiles - 1), 0))
    attl_spec = pl.BlockSpec((TM_BIG, MLA_WIDTH), lambda i: (jnp.maximum(i + t0 - ctx_tiles, 0), 0))
    if latent_only:
        out_spec = pl.BlockSpec((TM_BIG, D_MODEL), lambda i: (i, 0))
        out_rows = geom.B * geom.seq
    else:
        out_spec = tok(D_MODEL)
        out_rows = geom.n
    return pl.pallas_call(
        functools.partial(_mix_ffn_kernel, alpha=alpha, nct=ctx_tiles, t0=t0),
        grid=(geom.n // TM_BIG - t0,),
        in_specs=[
            tok(D_MODEL),
            mod(2), mod(3), mod(4), mod(5),
            tok(W), tok(W),
            tok(W), tok(W), vec(W), vec(W),
            pl.BlockSpec((W, W), lambda i: (0, 0)),
            attc_spec, attl_spec, tok(CONV_WIDTH), tok(GATE_SEG),
            wspec(MLA_WIDTH, D_MODEL), wspec(CONV_WIDTH, D_MODEL), wspec(W, D_MODEL),
            wspec(D_MODEL, D_MODEL),
            vec(D_MODEL), vec(D_MODEL),
            wspec(D_MODEL, 2 * D_FF), wspec(D_FF, D_MODEL),
            vec(D_MODEL), vec(D_MODEL),
        ],
        out_specs=out_spec,
        out_shape=jax.ShapeDtypeStruct((out_rows, D_MODEL), F32),
        compiler_params=_params(("parallel",)),
        name="mix_ffn",
    )(x, modv, modv, modv, modv, yf, yb, g, bonus, prm["rw_gn_g"], prm["rw_gn_b"], bavg,
      att_ctx, att_lat, conv, gate,
      prm["w_o_attn"], prm["w_o_conv"], prm["w_o_rwkv"], prm["w_out"],
      prm["ln1_g"], prm["ln1_b"], prm["ffn_w13"], prm["ffn_w2"], prm["ln2_g"], prm["ln2_b"])


_ROPE_SWAP = np.concatenate([np.arange(8, 16), np.arange(0, 8), np.arange(24, 32), np.arange(16, 24)])


def _rope_tables(geom):
    pos = jnp.arange(geom.seq)
    row = (pos // GRID_W).astype(F32)
    col = (pos % GRID_W).astype(F32)
    axis_dim = MLA_ROPE // 2
    inv = ROPE_BASE ** (-jnp.arange(0, axis_dim, 2, dtype=F32) / axis_dim)
    ar, ac = row[:, None] * inv, col[:, None] * inv
    cr, sr, cc, sc = jnp.cos(ar), jnp.sin(ar), jnp.cos(ac), jnp.sin(ac)
    cos32 = jnp.concatenate([cr, cr, cc, cc], -1)
    sin32 = jnp.concatenate([-sr, sr, -sc, sc], -1)
    ones = jnp.ones((geom.seq, MLA_NOPE), F32)
    zpad = jnp.zeros((geom.seq, HEAD_PAD - MLA_NOPE - MLA_ROPE), F32)
    cos_l = jnp.concatenate([ones, cos32, zpad], -1)
    sin_l = jnp.concatenate([jnp.zeros_like(ones), sin32, zpad], -1)
    cos_c = jnp.ones((TM_BIG, HEAD_PAD), F32)
    sin_c = jnp.zeros((TM_BIG, HEAD_PAD), F32)
    return jnp.concatenate([cos_c, cos_l], 0), jnp.concatenate([sin_c, sin_l], 0)


def _layout_weights(w_in, w_uq, w_ukv):
    L = w_in.shape[0]
    o_q, o_kv, o_kr = MLA_Q_LORA, MLA_Q_LORA + MLA_KV_LORA, MLA_Q_LORA + MLA_KV_LORA + MLA_ROPE
    w_in_b = w_in.astype(BF16)
    krope = w_in_b[:, :, o_kv:o_kr]
    zl = jnp.zeros((L, D_MODEL, MLA_NOPE), BF16)
    zr = jnp.zeros((L, D_MODEL, HEAD_PAD - MLA_NOPE - MLA_ROPE), BF16)
    w_in_p = jnp.concatenate(
        [w_in_b[:, :, :o_kv], zl, krope, zr, zl, krope[:, :, _ROPE_SWAP], zr, w_in_b[:, :, o_kr:]], -1)

    wq = w_uq.reshape(L, MLA_Q_LORA, MLA_HEADS, MLA_NOPE + MLA_ROPE)
    q_rope = wq[..., MLA_NOPE:]
    zq = jnp.zeros((L, MLA_Q_LORA, MLA_HEADS, HEAD_PAD - MLA_NOPE - MLA_ROPE), F32)
    zn = jnp.zeros((L, MLA_Q_LORA, MLA_HEADS, MLA_NOPE), F32)
    wq_p = jnp.concatenate([wq, zq], -1).reshape(L, MLA_Q_LORA, -1).astype(BF16)
    wqs_p = jnp.concatenate([zn, q_rope[..., _ROPE_SWAP], zq], -1).reshape(L, MLA_Q_LORA, -1).astype(BF16)

    wkv = w_ukv.reshape(L, MLA_KV_LORA, MLA_HEADS, MLA_NOPE + MLA_V)
    zk = jnp.zeros((L, MLA_KV_LORA, MLA_HEADS, HEAD_PAD - MLA_NOPE), F32)
    wk_p = jnp.concatenate([wkv[..., :MLA_NOPE], zk], -1).reshape(L, MLA_KV_LORA, -1).astype(BF16)
    zv = jnp.zeros((L, MLA_KV_LORA, MLA_HEADS, HEAD_PAD - MLA_V), F32)
    wv_p = jnp.concatenate([wkv[..., MLA_NOPE:], zv], -1).reshape(L, MLA_KV_LORA, -1).astype(BF16)
    return w_in_p, wq_p, wqs_p, wk_p, wv_p


def _head_block_ones():
    idx = np.arange(RW_WIDTH) // RW_HEAD
    return jnp.asarray((idx[:, None] == idx[None, :]).astype(np.float32), dtype=BF16)


def kernel(x, c, ctx, c_ctx, mod_w, mod_b, w_in, q_norm, w_uq, kv_norm, w_ukv, w_o_attn, conv_w, w_o_conv,
           rw_mu, rw_w0, rw_w_up, rw_a0, rw_a_up, rw_g_up, rw_k_k, rw_k_a, rw_r_k, rw_gn_g, rw_gn_b,
           w_o_rwkv, w_out, ln1_g, ln1_b, ffn_w13, ffn_w2, ln2_g, ln2_b):
    B, seq, _ = x.shape
    ctx_len = ctx.shape[1]
    L = mod_w.shape[0]
    geom = _Geom(B, ctx_len, seq)
    alpha = (2.0 * L) ** 0.25

    rows = -(-(B + 1) // HALO) * HALO
    c_all = jnp.concatenate([c, c_ctx[None, :], jnp.zeros((rows - B - 1, D_MODEL), F32)], 0)
    modv = _mod_vectors(c_all, mod_w, mod_b).reshape(L, rows, 1, -1)

    w_in_p, wq_p, wqs_p, wk_p, wv_p = _layout_weights(w_in, w_uq, w_ukv)
    cos_t, sin_t = _rope_tables(geom)
    bsum = _head_block_ones()
    vec3 = lambda a: a.reshape(L, 1, -1)
    prm = dict(
        rw_mu=vec3(rw_mu), rw_w0=rw_w0, rw_w_up=rw_w_up, rw_a0=rw_a0, rw_a_up=rw_a_up, rw_g_up=rw_g_up,
        rw_k_k=vec3(rw_k_k), rw_k_a=vec3(rw_k_a), rw_r_k=vec3(rw_r_k), conv_w=conv_w,
        rw_gn_g=vec3(rw_gn_g), rw_gn_b=vec3(rw_gn_b),
        w_o_attn=w_o_attn.astype(BF16), w_o_conv=w_o_conv.astype(BF16), w_o_rwkv=w_o_rwkv.astype(BF16),
        w_out=w_out.astype(BF16), ln1_g=vec3(ln1_g), ln1_b=vec3(ln1_b),
        ffn_w13=ffn_w13.astype(BF16), ffn_w2=ffn_w2.astype(BF16), ln2_g=vec3(ln2_g), ln2_b=vec3(ln2_b),
    )
    qn, kvn = vec3(q_norm), vec3(kv_norm)

    xs = jnp.concatenate([ctx.reshape(geom.nc, D_MODEL), x.reshape(B * seq, D_MODEL)], axis=0)
    for l in range(L):
        mla, conv, rw, gate = _in_proj(geom, l, xs, modv, w_in_p)
        q, k, v = _mla_proj(geom, l, mla, qn, kvn, wq_p, wqs_p, wk_p, wv_p, cos_t, sin_t)
        att_lat, att_ctx = _attention(geom, q, k, v)
        g, bonus, cv, gm, hm, rq, y0 = _features(geom, l, rw, conv, prm, bsum)
        yf, yb = _wkv_scan(geom, gm, hm, rq, y0)
        xs = _mix_ffn(geom, l, alpha, xs, modv, yf, yb, g, bonus, att_lat, att_ctx, cv, gate, prm, bsum,
                      latent_only=(l == L - 1))
    return xs.reshape(B, seq, D_MODEL)
```

```python
import functools

import numpy as np
import jax
import jax.numpy as jnp
from jax import lax
from jax.experimental import pallas as pl
from jax.experimental.pallas import tpu as pltpu

F32 = jnp.float32
BF16 = jnp.bfloat16

D_MODEL = 1024
GRID_W = 64
MLA_HEADS = 8
MLA_Q_LORA = 384
MLA_KV_LORA = 256
MLA_NOPE = 64
MLA_ROPE = 32
MLA_V = 64
MLA_WIDTH = MLA_HEADS * MLA_V
ROPE_BASE = 10000.0
CONV_WIDTH = 512
RW_HEADS = 8
RW_HEAD = 64
RW_WIDTH = RW_HEADS * RW_HEAD
RW_DECAY_LORA = 64
RW_ICLR_LORA = 64
RW_GATE_LORA = 128
RW_GN_EPS = 64e-5
RW_IN = 3 * RW_WIDTH + 2 * RW_DECAY_LORA + 2 * RW_ICLR_LORA + RW_GATE_LORA
N_BRANCH = 3
D_FF = 2816
LN_EPS = 1e-5
RMS_EPS = 1e-6

LANES = 128
HEAD_PAD = LANES
TM = 256
TM_BIG = 512
CHUNK = 64
PAIR = 2 * RW_HEAD
assert PAIR == LANES
PREP_INTERLEAVE = 4
HALO = 16
VMEM_LIMIT = 56 * 1024 * 1024

MLA_SEG = MLA_Q_LORA + MLA_KV_LORA + 2 * HEAD_PAD
CONV_SEG = 3 * CONV_WIDTH
GATE_SEG = N_BRANCH * D_MODEL
IN_SEGS = (MLA_SEG, CONV_SEG, RW_IN, GATE_SEG)
IN_OFFS = tuple(int(v) for v in np.cumsum((0,) + IN_SEGS))


def _bdot(a, b):
    return jnp.dot(a.astype(BF16), b.astype(BF16), preferred_element_type=F32)


def _split2(a):
    hi = a.astype(BF16)
    lo = (a - hi.astype(F32)).astype(BF16)
    return hi, lo


def _dot3(a, b, dims=None):
    ah, al = _split2(a)
    bh, bl = _split2(b)
    if dims is None:
        f = lambda u, v: jnp.dot(u, v, preferred_element_type=F32)
    else:
        f = lambda u, v: lax.dot_general(u, v, dims, preferred_element_type=F32)
    return f(ah, bh) + (f(ah, bl) + f(al, bh))


def _dot1(a, b, dims=None):
    a, b = a.astype(BF16), b.astype(BF16)
    if dims is None:
        return jnp.dot(a, b, preferred_element_type=F32)
    return lax.dot_general(a, b, dims, preferred_element_type=F32)


_dot_wkv = _dot1


def _dot_exact_rhs(a, b_bf16):
    a1, a2 = _split2(a)
    return jnp.dot(a1, b_bf16, preferred_element_type=F32) + jnp.dot(a2, b_bf16, preferred_element_type=F32)


def _sigmoid(x):
    return 0.5 * jnp.tanh(0.5 * x) + 0.5


def _layer_norm(x, g, b):
    mu = jnp.mean(x, -1, keepdims=True)
    xc = x - mu
    var = jnp.mean(xc * xc, -1, keepdims=True)
    return xc * lax.rsqrt(var + LN_EPS) * g + b


def _const_spec(shape, index_map):
    return pl.BlockSpec(shape, index_map, pipeline_mode=pl.Buffered(1))


def _params(sem):
    return pltpu.CompilerParams(dimension_semantics=sem, vmem_limit_bytes=VMEM_LIMIT)


def _mod_kernel(c_ref, w_ref, b_ref, o_ref):
    c = c_ref[...]
    s = c * _sigmoid(c)
    o_ref[...] = _dot3(s, w_ref[...]) + b_ref[...]


def _mod_vectors(c_all, mod_w, mod_b):
    L = mod_w.shape[0]
    R = c_all.shape[0]
    n = mod_w.shape[2] // D_MODEL
    return pl.pallas_call(
        _mod_kernel,
        grid=(L, n),
        in_specs=[
            pl.BlockSpec((R, D_MODEL), lambda l, j: (0, 0)),
            pl.BlockSpec((None, D_MODEL, D_MODEL), lambda l, j: (l, 0, j)),
            pl.BlockSpec((None, 1, D_MODEL), lambda l, j: (l, 0, j)),
        ],
        out_specs=pl.BlockSpec((None, R, D_MODEL), lambda l, j: (l, 0, j)),
        out_shape=jax.ShapeDtypeStruct((L, R, n * D_MODEL), F32),
        compiler_params=_params(("parallel", "parallel")),
        name="mod_vectors",
    )(c_all, mod_w, mod_b.reshape(L, 1, -1))


class _Geom:
    def __init__(self, B, ctx_len, seq):
        assert ctx_len % TM == 0 and seq % TM_BIG == 0 and seq % GRID_W == 0
        self.B, self.ctx_len, self.seq = B, ctx_len, seq
        self.tt = ctx_len + seq
        self.nct = ctx_len // TM
        self.nlt = seq // TM
        self.nt = self.nct + self.nlt
        self.nc = B * ctx_len
        self.n = B * self.tt
        assert self.nc % TM_BIG == 0 and self.nc % seq == 0

    def tile(self, b, t):
        return jnp.where(t < self.nct, b * self.nct + t, self.B * self.nct + b * self.nlt + (t - self.nct))


def _tok_spec(geom, width):
    return pl.BlockSpec((TM, width), lambda b, t: (geom.tile(b, t), 0))


def _in_proj_kernel(x_ref, sh_ref, sc_ref, w_ref, mla_ref, conv_ref, rw_ref, gate_ref):
    h = (x_ref[...] * (1.0 + sc_ref[...]) + sh_ref[...]).astype(BF16)
    for o_ref, lo, hi in zip((mla_ref, conv_ref, rw_ref, gate_ref), IN_OFFS[:-1], IN_OFFS[1:]):
        o_ref[...] = jnp.dot(h, w_ref[:, lo:hi], preferred_element_type=F32).astype(BF16)


def _in_proj(geom, layer, x, modv, w_in_p):
    ctx_tiles = geom.nc // TM_BIG
    per_row = geom.seq // TM_BIG

    def mod(j):
        row = lambda i: jnp.where(i < ctx_tiles, geom.B, (i - ctx_tiles) // per_row)
        return pl.BlockSpec((None, None, 1, D_MODEL), lambda i: (layer, row(i), 0, j))

    big = lambda w: pl.BlockSpec((TM_BIG, w), lambda i: (i, 0))
    return pl.pallas_call(
        _in_proj_kernel,
        grid=(geom.n // TM_BIG,),
        in_specs=[big(D_MODEL), mod(0), mod(1),
                  _const_spec((None, D_MODEL, IN_OFFS[-1]), lambda i: (layer, 0, 0))],
        out_specs=[big(w) for w in IN_SEGS],
        out_shape=[jax.ShapeDtypeStruct((geom.n, w), BF16) for w in IN_SEGS],
        compiler_params=_params(("parallel",)),
        name="in_proj",
    )(x, modv, modv, w_in_p)


def _mla_kernel(m_ref, qn_ref, kvn_ref, wq_ref, wqs_ref, wk_ref, wv_ref, cos_ref, sin_ref,
                q_ref, k_ref, v_ref):
    m = m_ref[...].astype(F32)
    cq = m[:, :MLA_Q_LORA]
    ckv = m[:, MLA_Q_LORA:MLA_Q_LORA + MLA_KV_LORA]
    kslab = m[:, MLA_Q_LORA + MLA_KV_LORA:MLA_Q_LORA + MLA_KV_LORA + HEAD_PAD]
    kslab_sw = m[:, MLA_Q_LORA + MLA_KV_LORA + HEAD_PAD:]
    cqn = (cq * lax.rsqrt(jnp.mean(cq * cq, -1, keepdims=True) + RMS_EPS) * qn_ref[...]).astype(BF16)
    ckvn = (ckv * lax.rsqrt(jnp.mean(ckv * ckv, -1, keepdims=True) + RMS_EPS) * kvn_ref[...]).astype(BF16)
    cos_t = cos_ref[...]
    sin_t = sin_ref[...]
    qa = jnp.dot(cqn, wq_ref[...], preferred_element_type=F32)
    qb = jnp.dot(cqn, wqs_ref[...], preferred_element_type=F32)
    kn = jnp.dot(ckvn, wk_ref[...], preferred_element_type=F32)
    kr = kslab * cos_t + kslab_sw * sin_t
    scale = (MLA_NOPE + MLA_ROPE) ** -0.5
    for h in range(MLA_HEADS):
        sl = slice(h * HEAD_PAD, (h + 1) * HEAD_PAD)
        q_ref[:, sl] = ((qa[:, sl] * cos_t + qb[:, sl] * sin_t) * scale).astype(BF16)
        k_ref[:, sl] = (kn[:, sl] + kr).astype(BF16)
    lane = lax.broadcasted_iota(jnp.int32, (1, MLA_HEADS * HEAD_PAD), 1) % HEAD_PAD
    v_ref[...] = (jnp.dot(ckvn, wv_ref[...], preferred_element_type=F32)
                  + (lane == MLA_V).astype(F32)).astype(BF16)


def _mla_proj(geom, layer, mla, q_norm, kv_norm, wq, wqs, wk, wv, cos_t, sin_t):
    hw = MLA_HEADS * HEAD_PAD
    lsel = lambda i: (layer, 0, 0)
    ctx_tiles = geom.nc // TM_BIG
    per_row = geom.seq // TM_BIG
    big = lambda w: pl.BlockSpec((TM_BIG, w), lambda i: (i, 0))
    rope = pl.BlockSpec((TM_BIG, HEAD_PAD), lambda i: (jnp.where(i < ctx_tiles, 0, 1 + (i - ctx_tiles) % per_row), 0))
    return pl.pallas_call(
        _mla_kernel,
        grid=(geom.n // TM_BIG,),
        in_specs=[
            big(MLA_SEG),
            pl.BlockSpec((None, 1, MLA_Q_LORA), lsel),
            pl.BlockSpec((None, 1, MLA_KV_LORA), lsel),
            _const_spec((None, MLA_Q_LORA, hw), lsel),
            _const_spec((None, MLA_Q_LORA, hw), lsel),
            _const_spec((None, MLA_KV_LORA, hw), lsel),
            _const_spec((None, MLA_KV_LORA, hw), lsel),
            rope, rope,
        ],
        out_specs=[big(hw)] * 3,
        out_shape=[jax.ShapeDtypeStruct((geom.n, hw), BF16)] * 3,
        compiler_params=_params(("parallel",)),
        name="mla_proj",
    )(mla, q_norm, kv_norm, wq, wqs, wk, wv, cos_t, sin_t)


HEADS_PER_STEP = 4
TQ = 1024


def _attend(q_ref, kv_refs, o_ref):
    heads = [slice(j * HEAD_PAD, (j + 1) * HEAD_PAD) for j in range(HEADS_PER_STEP)]

    def scores(hs):
        q = q_ref[:, hs]
        return [lax.dot_general(q, k_ref[:, hs], (((1,), (1,)), ((), ())), preferred_element_type=F32)
                for k_ref, _ in kv_refs]

    def probs(ss):
        m = functools.reduce(jnp.maximum, [jnp.max(s, -1, keepdims=True) for s in ss])
        return [jnp.exp(s - m).astype(BF16) for s in ss]

    def weighted(ps, hs):
        pv = sum(jnp.dot(p, v_ref[:, hs], preferred_element_type=F32) for p, (_, v_ref) in zip(ps, kv_refs))
        return pv[:, :MLA_V] / pv[:, MLA_V:MLA_V + 1]

    n = len(heads)
    ss = [scores(heads[0]), scores(heads[1])] + [None] * (n - 2)
    ps = [probs(ss[0])] + [None] * (n - 1)
    outs = []
    for j in range(n):
        if j + 2 < n:
            ss[j + 2] = scores(heads[j + 2])
        if j + 1 < n:
            ps[j + 1] = probs(ss[j + 1])
        outs.append(weighted(ps[j], heads[j]))
    o_ref[...] = jnp.concatenate(outs, -1).astype(BF16)


def _attn_kernel(ql_ref, qc_ref, kc_ref, vc_ref, kl_ref, vl_ref, ol_ref, oc_ref):
    _attend(ql_ref, [(kc_ref, vc_ref), (kl_ref, vl_ref)], ol_ref)

    @pl.when(pl.program_id(2) == 0)
    def _():
        _attend(qc_ref, [(kc_ref, vc_ref)], oc_ref)


def _attention(geom, q, k, v):
    qw = HEADS_PER_STEP * HEAD_PAD
    vw = HEADS_PER_STEP * MLA_V
    assert geom.seq % TQ == 0 and geom.nc % TQ == 0
    per_row = geom.seq // TQ
    lat0 = geom.nc // TQ
    seq0 = geom.nc // geom.seq
    ctx_spec = pl.BlockSpec((geom.ctx_len, qw), lambda b, h, t: (b, h))
    lat_spec = pl.BlockSpec((geom.seq, qw), lambda b, h, t: (seq0 + b, h))
    return pl.pallas_call(
        _attn_kernel,
        grid=(geom.B, MLA_HEADS // HEADS_PER_STEP, per_row),
        in_specs=[
            pl.BlockSpec((TQ, qw), lambda b, h, t: (lat0 + b * per_row + t, h)),
            ctx_spec, ctx_spec, ctx_spec, lat_spec, lat_spec,
        ],
        out_specs=[pl.BlockSpec((TQ, vw), lambda b, h, t: (b * per_row + t, h)),
                   pl.BlockSpec((geom.ctx_len, vw), lambda b, h, t: (b, h))],
        out_shape=[jax.ShapeDtypeStruct((geom.B * geom.seq, MLA_WIDTH), BF16),
                   jax.ShapeDtypeStruct((geom.nc, MLA_WIDTH), BF16)],
        compiler_params=_params(("parallel", "parallel", "arbitrary")),
        name="attn",
    )(q, q, k, v, k, v)


def _shift_prev(x, halo_row):
    rows = lax.broadcasted_iota(jnp.int32, x.shape, 0)
    return jnp.where(rows == 0, halo_row, pltpu.roll(x, 1, 0))


def _shift_next(x, halo_row):
    rows = lax.broadcasted_iota(jnp.int32, x.shape, 0)
    return jnp.where(rows == x.shape[0] - 1, halo_row, pltpu.roll(x, x.shape[0] - 1, 0))


def _feat_kernel(rw_ref, rwp_ref, rwn_ref, cv_ref, cvp_ref, cvn_ref,
                 mu_ref, w0_ref, wup_ref, a0_ref, aup_ref, gup_ref, kk_ref, ka_ref, rk_ref,
                 cw_ref, bsum_ref,
                 g_o, bonus_o, conv_o, gm_o, hm_o, rq_o, y0_o, *, nct, nt):
    t = pl.program_id(1)
    first = jnp.logical_or(t == 0, t == nct).astype(F32)
    last = jnp.logical_or(t == nct - 1, t == nt - 1).astype(F32)
    keep_prev = 1.0 - first
    keep_next = 1.0 - last
    W = RW_WIDTH

    def gated(ref):
        blk = ref[...].astype(F32)
        return blk[..., 2 * CONV_WIDTH:] * blk[..., :CONV_WIDTH]
    cblk = cv_ref[...].astype(F32)
    up = cblk[:, 2 * CONV_WIDTH:] * cblk[:, :CONV_WIDTH]
    up_p = _shift_prev(up, gated(cvp_ref)[HALO - 1:HALO] * keep_prev)
    up_n = _shift_next(up, gated(cvn_ref)[0:1] * keep_next)
    cw = cw_ref[...]
    conv_o[...] = (cblk[:, CONV_WIDTH:2 * CONV_WIDTH]
                   * (up_p * cw[0:1] + up * cw[1:2] + up_n * cw[2:3])).astype(BF16)

    p = rw_ref[...].astype(F32)
    p_prev = _shift_prev(p, rwp_ref[HALO - 1:HALO, :].astype(F32) * keep_prev)
    p_next = _shift_next(p, rwn_ref[0:1, :].astype(F32) * keep_next)
    p = p + (0.5 * (p_prev + p_next) - p) * mu_ref[...]
    r = p[:, 0:W]
    k = p[:, W:2 * W]
    v = p[:, 2 * W:3 * W]
    wd = p[:, 3 * W:3 * W + 2 * RW_DECAY_LORA]
    ad = p[:, 3 * W + 2 * RW_DECAY_LORA:3 * W + 2 * RW_DECAY_LORA + 2 * RW_ICLR_LORA]
    gd = p[:, 3 * W + 2 * RW_DECAY_LORA + 2 * RW_ICLR_LORA:]

    bsum = bsum_ref[...]
    kkf = k * kk_ref[...]
    kk = kkf * lax.rsqrt(jnp.maximum(_dot_exact_rhs(kkf * kkf, bsum), 1e-24))
    g_o[...] = _dot3(_sigmoid(gd), gup_ref[...]).astype(BF16)
    bonus_o[...] = (_dot_exact_rhs(r * k * rk_ref[...], bsum) * v).astype(BF16)
    for z in range(2):
        wdz = jnp.tanh(wd[:, z * RW_DECAY_LORA:(z + 1) * RW_DECAY_LORA])
        adz = ad[:, z * RW_ICLR_LORA:(z + 1) * RW_ICLR_LORA]
        tz = w0_ref[z:z + 1, :] + _dot3(wdz, wup_ref[z])
        lw = _sigmoid(tz) * (-float(np.exp(-0.5)))
        a = _sigmoid(a0_ref[z:z + 1, :] + _dot3(adz, aup_ref[z]))
        kd = k * (1.0 + (a - 1.0) * ka_ref[...])
        _wkv_chunk_reduce_pairs(z, r, v, kk, kd, kk * a, lw, gm_o, hm_o, rq_o, y0_o)


def _features(geom, layer, rw, conv, prm, bsum):
    nt, n = geom.nt, geom.n
    blocks_per_tile = TM // HALO
    last_halo_block = n // HALO - 1

    def prev_spec(w):
        return pl.BlockSpec((HALO, w), lambda b, t: (jnp.maximum(geom.tile(b, t) * blocks_per_tile - 1, 0), 0))

    def next_spec(w):
        return pl.BlockSpec(
            (HALO, w), lambda b, t: (jnp.minimum((geom.tile(b, t) + 1) * blocks_per_tile, last_halo_block), 0))

    lsel2 = lambda b, t: (layer, 0, 0)
    lsel3 = lambda b, t: (layer, 0, 0, 0)
    W = RW_WIDTH
    tok = _tok_spec(geom, W)
    dir_spec = pl.BlockSpec((2, TM, W), lambda b, t: (0, geom.tile(b, t), 0))
    tok_shape = jax.ShapeDtypeStruct((n, W), BF16)
    dir_shape = jax.ShapeDtypeStruct((2, n, W), BF16)
    return pl.pallas_call(
        functools.partial(_feat_kernel, nct=geom.nct, nt=nt),
        grid=(geom.B, nt),
        in_specs=[
            _tok_spec(geom, RW_IN), prev_spec(RW_IN), next_spec(RW_IN),
            _tok_spec(geom, CONV_SEG), prev_spec(CONV_SEG), next_spec(CONV_SEG),
            pl.BlockSpec((None, 1, RW_IN), lsel2),
            pl.BlockSpec((None, 2, W), lsel2),
            pl.BlockSpec((None, 2, RW_DECAY_LORA, W), lsel3),
            pl.BlockSpec((None, 2, W), lsel2),
            pl.BlockSpec((None, 2, RW_ICLR_LORA, W), lsel3),
            pl.BlockSpec((None, RW_GATE_LORA, W), lsel2),
            pl.BlockSpec((None, 1, W), lsel2),
            pl.BlockSpec((None, 1, W), lsel2),
            pl.BlockSpec((None, 1, W), lsel2),
            pl.BlockSpec((None, 3, CONV_WIDTH), lsel2),
            pl.BlockSpec((W, W), lambda b, t: (0, 0)),
        ],
        out_specs=[tok] * 3 + [dir_spec] * 4,
        out_shape=[tok_shape] * 3 + [dir_shape] * 4,
        compiler_params=_params(("parallel", "parallel")),
        name="feat_wkv_prep",
    )(rw, rw, rw, conv, conv, conv,
      prm["rw_mu"], prm["rw_w0"], prm["rw_w_up"], prm["rw_a0"], prm["rw_a_up"], prm["rw_g_up"],
      prm["rw_k_k"], prm["rw_k_a"], prm["rw_r_k"], prm["conv_w"], bsum)


def _block_diag(x, lo, hi):
    zero = jnp.zeros_like(x)
    return jnp.concatenate([jnp.where(lo, x, zero), jnp.where(hi, x, zero)], axis=0)


def _wkv_chunk_reduce_pairs(z, r_all, v_all, kk_all, kd_all, b_all, lw_all, g_o, h_o, rq_o, y0_o):
    fwd = z == 0
    C = CHUNK
    assert C == RW_HEAD
    ri = lax.broadcasted_iota(jnp.int32, (C, C), 0)
    ci = lax.broadcasted_iota(jnp.int32, (C, C), 1)
    incl_bf = ((ci <= ri) if fwd else (ci >= ri)).astype(F32).astype(BF16)
    rp = lax.broadcasted_iota(jnp.int32, (C, PAIR), 0)
    cp = lax.broadcasted_iota(jnp.int32, (C, PAIR), 1) % C
    strict = cp < rp if fwd else cp > rp
    incl = cp <= rp if fwd else cp >= rp
    eye = (cp == rp).astype(F32)
    lane1 = lax.broadcasted_iota(jnp.int32, (1, PAIR), 1)
    lane2 = lax.broadcasted_iota(jnp.int32, (1, 2 * PAIR), 1) % PAIR
    lo1, hi1 = lane1 < RW_HEAD, lane1 >= RW_HEAD
    lo2, hi2 = lane2 < RW_HEAD, lane2 >= RW_HEAD
    lane_even = lax.broadcasted_iota(jnp.int32, (1, RW_WIDTH), 1) % PAIR < RW_HEAD
    r2 = lax.broadcasted_iota(jnp.int32, (PAIR, PAIR), 0)
    c2 = lax.broadcasted_iota(jnp.int32, (PAIR, PAIR), 1)
    eye2 = r2 == c2
    nt_dims = (((1,), (1,)), ((), ()))
    npair = RW_WIDTH // PAIR
    pairs = [slice(p * PAIR, (p + 1) * PAIR) for p in range(npair)]

    def prologue(c):
        rows = slice(c * C, (c + 1) * C)
        lw = lw_all[rows, :]
        L = _dot_exact_rhs_lhs(incl_bf, lw)
        Ltot = L[C - 1:C, :] if fwd else L[0:1, :]
        enL = jnp.exp(-L)
        etail = jnp.exp(Ltot - L)
        kd = kd_all[rows, :]
        bb = b_all[rows, :]
        rt = r_all[rows, :] * jnp.exp(L)
        bt = (bb * enL).astype(BF16)
        kt = (kd * enL).astype(BF16)
        zero = jnp.zeros_like(bt)
        return dict(
            c=c, rows=rows, rt=rt,
            lhs=jnp.concatenate([kk_all[rows, :] * jnp.exp(L - lw), rt], axis=0).astype(BF16),
            rhs=jnp.concatenate([jnp.where(lane_even, bt, zero), jnp.where(lane_even, zero, bt),
                                 jnp.where(lane_even, kt, zero), jnp.where(lane_even, zero, kt)], axis=0),
            ktp_t=jnp.transpose(kd * etail).astype(BF16),
            btp_t=jnp.transpose(bb * etail).astype(BF16),
            ptot=jnp.exp(Ltot),
            vv=v_all[rows, :].astype(BF16))

    for c0 in range(0, TM // C, PREP_INTERLEAVE):
        chunks = [prologue(c) for c in range(c0, c0 + PREP_INTERLEAVE)]
        items = [(ch, ps) for ch in chunks for ps in pairs]
        aa = [_dot_wkv(ch["lhs"][:, ps], ch["rhs"][:, ps], nt_dims) for ch, ps in items]
        a_b = [jnp.where(strict, m[:C, :PAIR], 0.0) for m in aa]
        a_kk = [jnp.concatenate([jnp.where(strict, m[:C, PAIR:], 0.0), jnp.where(incl, m[C:, PAIR:], 0.0)], axis=0)
                .astype(BF16) for m in aa]
        a_rb = [jnp.where(incl, m[C:, :PAIR], 0.0).astype(BF16) for m in aa]
        bd = lambda p: _block_diag(p.astype(BF16), lo1, hi1)
        pw = [_dot_wkv(a, bd(a)) for a in a_b]
        xs = [eye - a for a in a_b]
        rounds = int(np.log2(C)) - 1
        for i in range(rounds):
            if i + 1 < rounds:
                both = [_dot_wkv(jnp.concatenate([p, x], axis=0), bd(p)) for p, x in zip(pw, xs)]
                pw = [m[:C] for m in both]
                xs = [x + m[C:] for x, m in zip(xs, both)]
            else:
                xs = [x + _dot_wkv(x, bd(p)) for x, p in zip(xs, pw)]
        avv = [_dot_wkv(m, _block_diag(ch["vv"][:, ps], lo1, hi1)) for m, (ch, ps) in zip(a_kk, items)]
        wu = [_dot_wkv(t, _block_diag(jnp.concatenate([ch["lhs"][:C, ps], av[:C].astype(BF16)], axis=1), lo2, hi2))
              .astype(BF16) for t, av, (ch, ps) in zip(xs, avv, items)]
        rbwu = [_dot_wkv(m, _block_diag(x, lo2, hi2)) for m, x in zip(a_rb, wu)]
        bwu = [_dot_wkv(ch["btp_t"][ps, :], x) for x, (ch, ps) in zip(wu, items)]
        kv = [_dot_wkv(ch["ktp_t"][ps, :], ch["vv"][:, ps]) for ch, ps in items]
        for av, rb, bw, kvp, (ch, ps) in zip(avv, rbwu, bwu, kv, items):
            rows = ch["rows"]
            rq_o[z, rows, ps] = (ch["rt"][:, ps] - rb[:, :PAIR]).astype(BF16)
            y0_o[z, rows, ps] = (av[C:] - rb[:, PAIR:]).astype(BF16)
            gfull = jnp.where(eye2, ch["ptot"][:, ps], 0.0) - bw[:, :PAIR]
            hfull = kvp - bw[:, PAIR:]
            g_o[z, rows, ps] = jnp.where(lo1, gfull[:C], gfull[C:]).astype(BF16)
            h_o[z, rows, ps] = jnp.where(lo1, hfull[:C], hfull[C:]).astype(BF16)


def _dot_exact_rhs_lhs(mask_bf16, x):
    x1, x2 = _split2(x)
    return jnp.dot(mask_bf16, x1, preferred_element_type=F32) + jnp.dot(mask_bf16, x2, preferred_element_type=F32)


def _wkv_scan_kernel(gf_ref, hf_ref, rqf_ref, y0f_ref, gb_ref, hb_ref, rqb_ref, y0b_ref,
                     yf_o, yb_o, state):
    C = CHUNK
    nch = TM // C

    @pl.when(pl.program_id(1) == 0)
    def _():
        state[...] = jnp.zeros_like(state)

    pairs = [slice(p * PAIR, (p + 1) * PAIR) for p in range(RW_WIDTH // PAIR)]
    lane = lax.broadcasted_iota(jnp.int32, (1, RW_WIDTH), 1) % PAIR
    lo, hi = lane < RW_HEAD, lane >= RW_HEAD
    dirs = ((gf_ref, hf_ref, rqf_ref, y0f_ref, yf_o), (gb_ref, hb_ref, rqb_ref, y0b_ref, yb_o))
    m = [state[0], state[1]]
    for c in range(nch):
        for z, (g_ref, h_ref, rq_ref, y0_ref, y_o) in enumerate(dirs):
            cc = c if z == 0 else nch - 1 - c
            rows = slice(cc * C, (cc + 1) * C)
            g = _block_diag(g_ref[rows, :], lo, hi)
            h = _block_diag(h_ref[rows, :], lo, hi)
            lhs = jnp.concatenate([rq_ref[rows, :], g], axis=0)
            mb = m[z].astype(BF16)
            ym = jnp.concatenate([jnp.dot(lhs[:, ps], mb[:, ps], preferred_element_type=F32) for ps in pairs], -1)
            y_o[rows, :] = (ym[:C] + y0_ref[rows, :]).astype(y_o.dtype)
            m[z] = ym[C:] + h
    state[0] = m[0]
    state[1] = m[1]


def _wkv_scan(geom, g, h, rq, y0):
    W = RW_WIDTH
    nt, nct = geom.nt, geom.nct

    def fwd_tile(b, t):
        return (0, geom.tile(b, t), 0)

    def bwd_tile(b, t):
        return (1, geom.tile(b, jnp.where(t < nct, nct - 1 - t, nt - 1 - (t - nct))), 0)

    fspec = pl.BlockSpec((None, TM, W), fwd_tile)
    bspec = pl.BlockSpec((None, TM, W), bwd_tile)
    return pl.pallas_call(
        _wkv_scan_kernel,
        grid=(geom.B, nt),
        in_specs=[fspec] * 4 + [bspec] * 4,
        out_specs=[pl.BlockSpec((TM, W), lambda b, t: fwd_tile(b, t)[1:]),
                   pl.BlockSpec((TM, W), lambda b, t: bwd_tile(b, t)[1:])],
        out_shape=[jax.ShapeDtypeStruct((geom.n, W), BF16)] * 2,
        scratch_shapes=[pltpu.VMEM((2, PAIR, W), F32)],
        compiler_params=_params(("parallel", "arbitrary")),
        name="wkv_scan",
    )(g, h, rq, y0, g, h, rq, y0)


def _mix_ffn_kernel(x_ref, g1_ref, sh2_ref, sc2_ref, g2_ref,
                    yf_ref, yb_ref, gg_ref, bonus_ref, gng_ref, gnb_ref, bavg_ref,
                    attc_ref, attl_ref, conv_ref, gate_ref,
                    woa_ref, woc_ref, wor_ref, wout_ref,
                    l1g_ref, l1b_ref, w13_ref, w2_ref, l2g_ref, l2b_ref,
                    o_ref, *, alpha, nct, t0):
    y = yf_ref[...].astype(F32) + yb_ref[...].astype(F32)
    bavg = bavg_ref[...]
    mu = _dot_exact_rhs(y, bavg) * (1.0 / RW_HEAD)
    yc = y - mu
    var = _dot_exact_rhs(yc * yc, bavg) * (1.0 / RW_HEAD)
    yn = yc * lax.rsqrt(var + RW_GN_EPS) * gng_ref[...] + gnb_ref[...]
    rwo = (yn + bonus_ref[...]) * gg_ref[...]

    gl = gate_ref[...].astype(F32)
    att = jnp.where(pl.program_id(0) + t0 < nct, attc_ref[...], attl_ref[...])
    merged = (_sigmoid(gl[:, 0:D_MODEL]) * _bdot(att, woa_ref[...])
              + _sigmoid(gl[:, D_MODEL:2 * D_MODEL]) * _bdot(conv_ref[...], woc_ref[...])
              + _sigmoid(gl[:, 2 * D_MODEL:]) * _bdot(rwo, wor_ref[...]))
    o = _bdot(merged, wout_ref[...])
    x = x_ref[...]
    x1 = _layer_norm(alpha * x + g1_ref[...] * o, l1g_ref[...], l1b_ref[...])

    hmod = x1 * (1.0 + sc2_ref[...]) + sh2_ref[...]
    ug = _bdot(hmod, w13_ref[...])
    u = ug[:, :D_FF]
    f = _bdot(u * _sigmoid(u) * ug[:, D_FF:], w2_ref[...])
    o_ref[...] = _layer_norm(alpha * x1 + g2_ref[...] * f, l2g_ref[...], l2b_ref[...])


def _mix_ffn(geom, layer, alpha, x, modv, yf, yb, g, bonus, att_lat, att_ctx, conv, gate, prm, bavg, latent_only):
    W = RW_WIDTH
    ctx_tiles = geom.nc // TM_BIG
    per_row = geom.seq // TM_BIG
    t0 = ctx_tiles if latent_only else 0
    lsel = lambda i: (layer, 0, 0)
    vec = lambda w: pl.BlockSpec((None, 1, w), lsel)
    wspec = lambda r, c: _const_spec((None, r, c), lsel)
    tok = lambda w: pl.BlockSpec((TM_BIG, w), lambda i: (i + t0, 0))
    mod_row = lambda i: jnp.where(i + t0 < ctx_tiles, geom.B, (i + t0 - ctx_tiles) // per_row)
    mod = lambda j: pl.BlockSpec((None, None, 1, D_MODEL), lambda i: (layer, mod_row(i), 0, j))
    attc_spec = pl.BlockSpec((TM_BIG, MLA_WIDTH), lambda i: (jnp.minimum(i + t0, ctx_tiles - 1), 0))
    attl_spec = pl.BlockSpec((TM_BIG, MLA_WIDTH), lambda i: (jnp.maximum(i + t0 - ctx_tiles, 0), 0))
    if latent_only:
        out_spec = pl.BlockSpec((TM_BIG, D_MODEL), lambda i: (i, 0))
        out_rows = geom.B * geom.seq
    else:
        out_spec = tok(D_MODEL)
        out_rows = geom.n
    return pl.pallas_call(
        functools.partial(_mix_ffn_kernel, alpha=alpha, nct=ctx_tiles, t0=t0),
        grid=(geom.n // TM_BIG - t0,),
        in_specs=[
            tok(D_MODEL),
            mod(2), mod(3), mod(4), mod(5),
            tok(W), tok(W),
            tok(W), tok(W), vec(W), vec(W),
            pl.BlockSpec((W, W), lambda i: (0, 0)),
            attc_spec, attl_spec, tok(CONV_WIDTH), tok(GATE_SEG),
            wspec(MLA_WIDTH, D_MODEL), wspec(CONV_WIDTH, D_MODEL), wspec(W, D_MODEL),
            wspec(D_MODEL, D_MODEL),
            vec(D_MODEL), vec(D_MODEL),
            wspec(D_MODEL, 2 * D_FF), wspec(D_FF, D_MODEL),
            vec(D_MODEL), vec(D_MODEL),
        ],
        out_specs=out_spec,
        out_shape=jax.ShapeDtypeStruct((out_rows, D_MODEL), F32),
        compiler_params=_params(("parallel",)),
        name="mix_ffn",
    )(x, modv, modv, modv, modv, yf, yb, g, bonus, prm["rw_gn_g"], prm["rw_gn_b"], bavg,
      att_ctx, att_lat, conv, gate,
      prm["w_o_attn"], prm["w_o_conv"], prm["w_o_rwkv"], prm["w_out"],
      prm["ln1_g"], prm["ln1_b"], prm["ffn_w13"], prm["ffn_w2"], prm["ln2_g"], prm["ln2_b"])


_ROPE_SWAP = np.concatenate([np.arange(8, 16), np.arange(0, 8), np.arange(24, 32), np.arange(16, 24)])


def _rope_tables(geom):
    pos = jnp.arange(geom.seq)
    row = (pos // GRID_W).astype(F32)
    col = (pos % GRID_W).astype(F32)
    axis_dim = MLA_ROPE // 2
    inv = ROPE_BASE ** (-jnp.arange(0, axis_dim, 2, dtype=F32) / axis_dim)
    ar, ac = row[:, None] * inv, col[:, None] * inv
    cr, sr, cc, sc = jnp.cos(ar), jnp.sin(ar), jnp.cos(ac), jnp.sin(ac)
    cos32 = jnp.concatenate([cr, cr, cc, cc], -1)
    sin32 = jnp.concatenate([-sr, sr, -sc, sc], -1)
    ones = jnp.ones((geom.seq, MLA_NOPE), F32)
    zpad = jnp.zeros((geom.seq, HEAD_PAD - MLA_NOPE - MLA_ROPE), F32)
    cos_l = jnp.concatenate([ones, cos32, zpad], -1)
    sin_l = jnp.concatenate([jnp.zeros_like(ones), sin32, zpad], -1)
    cos_c = jnp.ones((TM_BIG, HEAD_PAD), F32)
    sin_c = jnp.zeros((TM_BIG, HEAD_PAD), F32)
    return jnp.concatenate([cos_c, cos_l], 0), jnp.concatenate([sin_c, sin_l], 0)


def _layout_weights(w_in, w_uq, w_ukv):
    L = w_in.shape[0]
    o_q, o_kv, o_kr = MLA_Q_LORA, MLA_Q_LORA + MLA_KV_LORA, MLA_Q_LORA + MLA_KV_LORA + MLA_ROPE
    w_in_b = w_in.astype(BF16)
    krope = w_in_b[:, :, o_kv:o_kr]
    zl = jnp.zeros((L, D_MODEL, MLA_NOPE), BF16)
    zr = jnp.zeros((L, D_MODEL, HEAD_PAD - MLA_NOPE - MLA_ROPE), BF16)
    w_in_p = jnp.concatenate(
        [w_in_b[:, :, :o_kv], zl, krope, zr, zl, krope[:, :, _ROPE_SWAP], zr, w_in_b[:, :, o_kr:]], -1)

    wq = w_uq.reshape(L, MLA_Q_LORA, MLA_HEADS, MLA_NOPE + MLA_ROPE)
    q_rope = wq[..., MLA_NOPE:]
    zq = jnp.zeros((L, MLA_Q_LORA, MLA_HEADS, HEAD_PAD - MLA_NOPE - MLA_ROPE), F32)
    zn = jnp.zeros((L, MLA_Q_LORA, MLA_HEADS, MLA_NOPE), F32)
    wq_p = jnp.concatenate([wq, zq], -1).reshape(L, MLA_Q_LORA, -1).astype(BF16)
    wqs_p = jnp.concatenate([zn, q_rope[..., _ROPE_SWAP], zq], -1).reshape(L, MLA_Q_LORA, -1).astype(BF16)

    wkv = w_ukv.reshape(L, MLA_KV_LORA, MLA_HEADS, MLA_NOPE + MLA_V)
    zk = jnp.zeros((L, MLA_KV_LORA, MLA_HEADS, HEAD_PAD - MLA_NOPE), F32)
    wk_p = jnp.concatenate([wkv[..., :MLA_NOPE], zk], -1).reshape(L, MLA_KV_LORA, -1).astype(BF16)
    zv = jnp.zeros((L, MLA_KV_LORA, MLA_HEADS, HEAD_PAD - MLA_V), F32)
    wv_p = jnp.concatenate([wkv[..., MLA_NOPE:], zv], -1).reshape(L, MLA_KV_LORA, -1).astype(BF16)
    return w_in_p, wq_p, wqs_p, wk_p, wv_p


def _head_block_ones():
    idx = np.arange(RW_WIDTH) // RW_HEAD
    return jnp.asarray((idx[:, None] == idx[None, :]).astype(np.float32), dtype=BF16)


def kernel(x, c, ctx, c_ctx, mod_w, mod_b, w_in, q_norm, w_uq, kv_norm, w_ukv, w_o_attn, conv_w, w_o_conv,
           rw_mu, rw_w0, rw_w_up, rw_a0, rw_a_up, rw_g_up, rw_k_k, rw_k_a, rw_r_k, rw_gn_g, rw_gn_b,
           w_o_rwkv, w_out, ln1_g, ln1_b, ffn_w13, ffn_w2, ln2_g, ln2_b):
    B, seq, _ = x.shape
    ctx_len = ctx.shape[1]
    L = mod_w.shape[0]
    geom = _Geom(B, ctx_len, seq)
    alpha = (2.0 * L) ** 0.25

    rows = -(-(B + 1) // HALO) * HALO
    c_all = jnp.concatenate([c, c_ctx[None, :], jnp.zeros((rows - B - 1, D_MODEL), F32)], 0)
    modv = _mod_vectors(c_all, mod_w, mod_b).reshape(L, rows, 1, -1)

    w_in_p, wq_p, wqs_p, wk_p, wv_p = _layout_weights(w_in, w_uq, w_ukv)
    cos_t, sin_t = _rope_tables(geom)
    bsum = _head_block_ones()
    vec3 = lambda a: a.reshape(L, 1, -1)
    prm = dict(
        rw_mu=vec3(rw_mu), rw_w0=rw_w0, rw_w_up=rw_w_up, rw_a0=rw_a0, rw_a_up=rw_a_up, rw_g_up=rw_g_up,
        rw_k_k=vec3(rw_k_k), rw_k_a=vec3(rw_k_a), rw_r_k=vec3(rw_r_k), conv_w=conv_w,
        rw_gn_g=vec3(rw_gn_g), rw_gn_b=vec3(rw_gn_b),
        w_o_attn=w_o_attn.astype(BF16), w_o_conv=w_o_conv.astype(BF16), w_o_rwkv=w_o_rwkv.astype(BF16),
        w_out=w_out.astype(BF16), ln1_g=vec3(ln1_g), ln1_b=vec3(ln1_b),
        ffn_w13=ffn_w13.astype(BF16), ffn_w2=ffn_w2.astype(BF16), ln2_g=vec3(ln2_g), ln2_b=vec3(ln2_b),
    )
    qn, kvn = vec3(q_norm), vec3(kv_norm)

    xs = jnp.concatenate([ctx.reshape(geom.nc, D_MODEL), x.reshape(B * seq, D_MODEL)], axis=0)
    for l in range(L):
        mla, conv, rw, gate = _in_proj(geom, l, xs, modv, w_in_p)
        q, k, v = _mla_proj(geom, l, mla, qn, kvn, wq_p, wqs_p, wk_p, wv_p, cos_t, sin_t)
        att_lat, att_ctx = _attention(geom, q, k, v)
        g, bonus, cv, gm, hm, rq, y0 = _features(geom, l, rw, conv, prm, bsum)
        yf, yb = _wkv_scan(geom, gm, hm, rq, y0)
        xs = _mix_ffn(geom, l, alpha, xs, modv, yf, yb, g, bonus, att_lat, att_ctx, cv, gate, prm, bsum,
                      latent_only=(l == L - 1))
    return xs.reshape(B, seq, D_MODEL)
```
